```python
import math
import jax, jax.numpy as jnp
from jax import lax
import numpy as np

D_MODEL = 1024
BATCH = 4
SEQ = 4096
DEPTH = 4

N_A_LAYERS = DEPTH // 2
N_B_LAYERS = DEPTH - N_A_LAYERS

EXPAND = 2
D_A = EXPAND * D_MODEL
POOL_WINDOWS = (2, 4, 8, 16)
N_POOL_GROUPS = len(POOL_WINDOWS)
POOL_GW = D_A // N_POOL_GROUPS

ATTN_PAIRS = ((128, 1), (512, 4), (2048, 16))
N_GROUPS = len(ATTN_PAIRS)
N_HEADS = 16
HEAD_DIM = D_MODEL // N_HEADS
D_B = N_HEADS * HEAD_DIM
BLK = 128

DN_ALPHA = (2 * DEPTH) ** 0.25
DN_BETA = (8 * DEPTH) ** -0.25
LN_EPS = 1e-5
NEG_INF = -1e30

kernel_name = "yoco_pool_dilated_attn_deepnorm"


def layer_norm(x, g, b):
    xf = x.astype(jnp.float32)
    mu = jnp.mean(xf, axis=-1, keepdims=True)
    var = jnp.mean(jnp.square(xf - mu), axis=-1, keepdims=True)
    y = (xf - mu) * lax.rsqrt(var + LN_EPS) * g.astype(jnp.float32) + b.astype(jnp.float32)
    return y.astype(x.dtype)


def alibi_slopes():
    n = N_GROUPS * N_HEADS
    i = jnp.arange(1, n + 1, dtype=jnp.float32)
    return (2.0 ** (-8.0 * i / n)).reshape(N_GROUPS, N_HEADS)


def pool_mixer(h, w_in, w_grp, scale, w_out):
    B, S, _ = h.shape
    proj = h @ w_in
    u, gate = proj[..., :D_A], proj[..., D_A:]
    uf = u.astype(jnp.float32).reshape(B, S, N_POOL_GROUPS, POOL_GW)
    pos = jnp.arange(S)
    pooled = []
    for g, w in enumerate(POOL_WINDOWS):
        ug = uf[:, :, g]
        cs = jnp.cumsum(ug, axis=1)
        shifted = jnp.pad(cs, ((0, 0), (w, 0), (0, 0)))[:, :S]
        cnt = jnp.minimum(pos + 1, w).astype(jnp.float32)
        pooled.append((cs - shifted) / cnt[None, :, None] - ug)
    pooled = jnp.stack(pooled, axis=2)
    mixed = jnp.einsum('bsgc,gce->bsge', pooled, w_grp.astype(jnp.float32))
    mixed = (mixed.reshape(B, S, D_A) * scale.astype(jnp.float32)).astype(h.dtype)
    return (mixed * jax.nn.silu(gate)) @ w_out


def dilated_window_attention(q, k, v, window, dilation, slopes):
    B, S, H, dh = q.shape
    L = S // dilation
    nblk = -(-L // BLK)
    Lp = nblk * BLK
    win_sub = window // dilation

    def to_sub(t):
        t = t.reshape(B, L, dilation, H, dh)
        t = jnp.pad(t, ((0, 0), (0, Lp - L), (0, 0), (0, 0), (0, 0)))
        return t.reshape(B, nblk, BLK, dilation, H, dh)

    qb, kb, vb = to_sub(q), to_sub(k), to_sub(v)

    def with_prev(t):
        prev = jnp.concatenate([jnp.zeros_like(t[:, :1]), t[:, :-1]], axis=1)
        return jnp.concatenate([prev, t], axis=2)

    kk, vv = with_prev(kb), with_prev(vb)
    scores = jnp.einsum('bnqrhd,bnkrhd->bnrhqk', qb, kk) * (HEAD_DIM ** -0.5)

    a = jnp.arange(BLK)[:, None]
    c = jnp.arange(2 * BLK)[None, :]
    rel = BLK + a - c
    key_pos = (jnp.arange(nblk)[:, None, None] - 1) * BLK + c[None]
    valid = (rel >= 0)[None] & (rel <= win_sub)[None] & (key_pos >= 0)
    bias = -slopes[:, None, None] * (dilation * rel).astype(jnp.float32)[None]
    scores = jnp.where(valid[None, :, None, None], scores + bias[None, None, None], NEG_INF)

    lse = jax.nn.logsumexp(scores, axis=-1)
    p = jnp.exp(scores - lse[..., None])
    out = jnp.einsum('bnrhqk,bnkrhd->bnqrhd', p, vv)
    out = out.reshape(B, Lp, dilation, H, dh)[:, :L].reshape(B, S, H, dh)
    lse = jnp.transpose(lse, (0, 1, 4, 2, 3)).reshape(B, Lp, dilation, H)[:, :L].reshape(B, S, H)
    return out, lse


def dilated_mixer(h, w_in, w_out, k_sh, v_sh, slopes):
    B, S, _ = h.shape
    proj = h @ w_in
    q_all = proj[..., :N_GROUPS * D_B].astype(jnp.float32).reshape(B, S, N_GROUPS, N_HEADS, HEAD_DIM)
    gate = proj[..., N_GROUPS * D_B:]
    kf = k_sh.astype(jnp.float32)
    vf = v_sh.astype(jnp.float32)
    outs, lses = [], []
    for g, (window, dilation) in enumerate(ATTN_PAIRS):
        o, l = dilated_window_attention(q_all[:, :, g], kf[:, :, g], vf[:, :, g], window, dilation, slopes[g])
        outs.append(o)
        lses.append(l)
    wts = jax.nn.softmax(jnp.stack(lses, axis=0), axis=0)
    merged = jnp.sum(wts[..., None] * jnp.stack(outs, axis=0), axis=0)
    merged = merged.reshape(B, S, D_B).astype(h.dtype)
    return (merged * jax.nn.silu(gate)) @ w_out


def setup_inputs(seed: int = 0) -> dict:
    key = jax.random.key(seed)
    ks = jax.random.split(key, 12)
    f32 = jnp.float32
    x = jax.random.normal(ks[0], (BATCH, SEQ, D_MODEL), f32)
    w_in_a = jax.random.normal(ks[1], (N_A_LAYERS, D_MODEL, 2 * D_A), f32) * D_MODEL ** -0.5
    w_grp_a = jax.random.normal(ks[2], (N_A_LAYERS, N_POOL_GROUPS, POOL_GW, POOL_GW), f32) * POOL_GW ** -0.5
    scale_a = 1.0 + 0.1 * jax.random.normal(ks[3], (N_A_LAYERS, D_A), f32)
    w_out_a = jax.random.normal(ks[4], (N_A_LAYERS, D_A, D_MODEL), f32) * (D_A ** -0.5) * DN_BETA
    w_kv = jax.random.normal(ks[5], (D_MODEL, 2 * N_GROUPS * D_B), f32) * D_MODEL ** -0.5
    w_in_b = jax.random.normal(ks[6], (N_B_LAYERS, D_MODEL, (N_GROUPS + 1) * D_B), f32) * D_MODEL ** -0.5
    w_out_b = jax.random.normal(ks[7], (N_B_LAYERS, D_B, D_MODEL), f32) * (D_B ** -0.5) * DN_BETA
    ln_g = 1.0 + 0.02 * jax.random.normal(ks[8], (DEPTH, D_MODEL), f32)
    ln_b = 0.02 * jax.random.normal(ks[9], (DEPTH, D_MODEL), f32)
    return {"x": x, "w_in_a": w_in_a, "w_grp_a": w_grp_a, "scale_a": scale_a, "w_out_a": w_out_a,
            "w_kv": w_kv, "w_in_b": w_in_b, "w_out_b": w_out_b, "ln_g": ln_g, "ln_b": ln_b}


def reference(x, w_in_a, w_grp_a, scale_a, w_out_a, w_kv, w_in_b, w_out_b, ln_g, ln_b):
    B, S, _ = x.shape
    slopes = alibi_slopes()
    h = x
    k_sh = None
    v_sh = None
    for l in range(DEPTH):
        if l < N_A_LAYERS:
            y = pool_mixer(h, w_in_a[l], w_grp_a[l], scale_a[l], w_out_a[l])
        else:
            j = l - N_A_LAYERS
            y = dilated_mixer(h, w_in_b[j], w_out_b[j], k_sh, v_sh, slopes)
        h = layer_norm(DN_ALPHA * h + y, ln_g[l], ln_b[l])
        if l == N_A_LAYERS - 1:
            kv = h @ w_kv
            k_sh = kv[..., :N_GROUPS * D_B].reshape(B, S, N_GROUPS, N_HEADS, HEAD_DIM)
            v_sh = kv[..., N_GROUPS * D_B:].reshape(B, S, N_GROUPS, N_HEADS, HEAD_DIM)
    return h
```

```python
import functools
import math

import jax
import jax.numpy as jnp
from jax import lax
from jax.experimental import pallas as pl
from jax.experimental.pallas import tpu as pltpu

D_MODEL = 1024
DEPTH = 4
N_A_LAYERS = DEPTH // 2
D_A = 2 * D_MODEL
POOL_WINDOWS = (2, 4, 8, 16)
N_POOL_GROUPS = len(POOL_WINDOWS)
POOL_GW = D_A // N_POOL_GROUPS
ATTN_PAIRS = ((128, 1), (512, 4), (2048, 16))
N_GROUPS = len(ATTN_PAIRS)
N_HEADS = 16
HEAD_DIM = D_MODEL // N_HEADS
D_B = N_HEADS * HEAD_DIM
BLK = 128
DN_ALPHA = (2 * DEPTH) ** 0.25
LN_EPS = 1e-5
NEG_INF = -1e30

LANES = 128
N_PAIRS = N_HEADS // 2
POOL_HALO = 16
VMEM_LIMIT_BYTES = 56 * 1024 * 1024

F32 = jnp.float32
BF16 = jnp.bfloat16


def _dot(a, b):
    return jnp.dot(a, b, preferred_element_type=F32)


def _silu(x):
    return x * (1.0 / (1.0 + jnp.exp(-x)))


def _deepnorm_ln(h, y, g, b):
    z = DN_ALPHA * h + y
    mu = jnp.mean(z, axis=-1, keepdims=True)
    zc = z - mu
    var = jnp.mean(zc * zc, axis=-1, keepdims=True)
    return zc * lax.rsqrt(var + LN_EPS) * g + b


def _const_spec(shape):
    nd = len(shape)
    return pl.BlockSpec(shape, lambda *_: (0,) * nd, pipeline_mode=pl.Buffered(1))


def _pool_layer_kernel(h_ref, w_in_ref, w_grp_ref, scale_ref, w_out_ref, g_ref, b_ref,
                       o_ref, uext_ref, *, tile):
    s = pl.program_id(1)

    @pl.when(s == 0)
    def _():
        uext_ref[0:POOL_HALO, :] = jnp.zeros((POOL_HALO, D_A), F32)

    h = h_ref[0]
    hb = h.astype(BF16)
    pos = s * tile + lax.broadcasted_iota(jnp.int32, (tile, 1), 0)
    y = jnp.zeros((tile, D_MODEL), F32)
    for g, w in enumerate(POOL_WINDOWS):
        c0 = g * POOL_GW
        u = _dot(hb, w_in_ref[:, c0:c0 + POOL_GW])
        uext_ref[POOL_HALO:POOL_HALO + tile, c0:c0 + POOL_GW] = u
        acc = u
        for k in range(1, w):
            acc = acc + uext_ref[POOL_HALO - k:POOL_HALO - k + tile, c0:c0 + POOL_GW]
        inv_cnt = 1.0 / jnp.minimum(pos + 1, w).astype(F32)
        pooled = acc * inv_cnt - u
        mixed = _dot(pooled.astype(BF16), w_grp_ref[g]) * scale_ref[:, c0:c0 + POOL_GW]
        gate = _dot(hb, w_in_ref[:, D_A + c0:D_A + c0 + POOL_GW])
        act = mixed * _silu(gate)
        y = y + _dot(act.astype(BF16), w_out_ref[c0:c0 + POOL_GW, :])
    uext_ref[0:POOL_HALO, :] = uext_ref[tile:tile + POOL_HALO, :]
    o_ref[0] = _deepnorm_ln(h, y, g_ref[...], b_ref[...])


def _pool_layer(h, w_in, w_grp, scale, w_out, ln_g, ln_b, *, tile=512):
    B, S, D = h.shape
    kern = functools.partial(_pool_layer_kernel, tile=tile)
    return pl.pallas_call(
        kern,
        grid=(B, S // tile),
        in_specs=[
            pl.BlockSpec((1, tile, D), lambda b, s: (b, s, 0)),
            _const_spec(w_in.shape),
            _const_spec(w_grp.shape),
            _const_spec(scale.shape),
            _const_spec(w_out.shape),
            _const_spec(ln_g.shape),
            _const_spec(ln_b.shape),
        ],
        out_specs=pl.BlockSpec((1, tile, D), lambda b, s: (b, s, 0)),
        out_shape=jax.ShapeDtypeStruct((B, S, D), F32),
        scratch_shapes=[pltpu.VMEM((POOL_HALO + tile, D_A), F32)],
        compiler_params=pltpu.CompilerParams(
            dimension_semantics=("arbitrary", "arbitrary"),
            vmem_limit_bytes=VMEM_LIMIT_BYTES),
        name="pool_layer",
    )(h, w_in, w_grp, scale, w_out, ln_g, ln_b)


def _proj_kernel(x_ref, w_ref, *out_refs, n_bf16, n_f32, chunk):
    xb = x_ref[0].astype(BF16)
    for c0 in range(0, n_bf16, chunk):
        out_refs[0][0, :, c0:c0 + chunk] = _dot(xb, w_ref[:, c0:c0 + chunk]).astype(BF16)
    for c0 in range(0, n_f32, chunk):
        out_refs[-1][0, :, c0:c0 + chunk] = _dot(xb, w_ref[:, n_bf16 + c0:n_bf16 + c0 + chunk])


def _project(x, w, *, n_bf16, tile=512, chunk=1024):
    B, S, D = x.shape
    n_f32 = w.shape[1] - n_bf16
    out_shape = [jax.ShapeDtypeStruct((B, S, n_bf16), BF16)]
    out_specs = [pl.BlockSpec((1, tile, n_bf16), lambda b, s: (b, s, 0))]
    if n_f32:
        out_shape.append(jax.ShapeDtypeStruct((B, S, n_f32), F32))
        out_specs.append(pl.BlockSpec((1, tile, n_f32), lambda b, s: (b, s, 0)))
    kern = functools.partial(_proj_kernel, n_bf16=n_bf16, n_f32=n_f32, chunk=chunk)
    return pl.pallas_call(
        kern,
        grid=(B, S // tile),
        in_specs=[pl.BlockSpec((1, tile, D), lambda b, s: (b, s, 0)), _const_spec(w.shape)],
        out_specs=out_specs,
        out_shape=out_shape,
        compiler_params=pltpu.CompilerParams(
            dimension_semantics=("arbitrary", "arbitrary"),
            vmem_limit_bytes=VMEM_LIMIT_BYTES),
        name="project",
    )(x, w)


def _attn_bias(group, dilation):
    n = N_GROUPS * N_HEADS
    i = jnp.arange(1, n + 1, dtype=F32)
    slopes = (2.0 ** (-8.0 * i / n)).reshape(N_GROUPS, N_HEADS)[group]
    a = jnp.arange(BLK)[:, None]
    c = jnp.arange(2 * BLK)[None, :]
    rel = BLK + a - c
    win_sub = ATTN_PAIRS[group][0] // dilation
    valid = (rel >= 0) & (rel <= win_sub)
    bias = -slopes[:, None, None] * (dilation * rel).astype(F32)[None]
    later = jnp.where(valid[None], bias, NEG_INF)
    first = jnp.where((valid & (c >= BLK))[None], bias, NEG_INF)
    return jnp.stack([later, first], axis=0)


def _attn_kernel(q_ref, kp_ref, kc_ref, vp_ref, vc_ref, bias_ref, o_ref, st_ref):
    n = pl.program_id(2)
    bsel = (n == 0).astype(jnp.int32)
    lane = lax.broadcasted_iota(jnp.int32, (1, LANES), 1)
    lo = lane < HEAD_DIM
    ones_lo = jnp.broadcast_to(jnp.where(lo, 1.0, 0.0).astype(BF16), (2 * BLK, LANES))
    ones_hi = jnp.broadcast_to(jnp.where(lo, 0.0, 1.0).astype(BF16), (2 * BLK, LANES))
    stats = jnp.zeros((BLK, LANES), F32)
    for p in range(N_PAIRS):
        cols = slice(p * LANES, (p + 1) * LANES)
        qp = q_ref[0, :, cols]
        kp = jnp.concatenate([kp_ref[0, :, cols], kc_ref[0, :, cols]], axis=0)
        vp = jnp.concatenate([vp_ref[0, :, cols], vc_ref[0, :, cols]], axis=0)
        acc = jnp.zeros((BLK, 2 * LANES), F32)
        m_pair = jnp.zeros((BLK, LANES), F32)
        for hh in range(2):
            msk = lo if hh == 0 else jnp.logical_not(lo)
            qm = jnp.where(msk, qp, jnp.zeros_like(qp))
            sc = lax.dot_general(qm, kp, (((1,), (1,)), ((), ())), preferred_element_type=F32)
            sc = sc + bias_ref[bsel, 2 * p + hh]
            m = jnp.max(sc, axis=-1, keepdims=True)
            pe = jnp.exp(sc - m).astype(BF16)
            vm = jnp.where(msk, vp, jnp.zeros_like(vp))
            rhs = jnp.concatenate([vm, ones_lo if hh == 0 else ones_hi], axis=1)
            acc = acc + _dot(pe, rhs)
            m_pair = jnp.where(msk, m, m_pair)
        denom = acc[:, LANES:]
        o_ref[0, :, cols] = (acc[:, :LANES] / denom).astype(BF16)
        lse = m_pair + jnp.log(denom)
        stats = jnp.where((lane & (HEAD_DIM - 1)) == p, lse, stats)
    st_ref[0] = stats


def _attention_group(q, kv, group):
    B, S, _ = q.shape
    _, d = ATTN_PAIRS[group]
    L = S // d
    nblk = L // BLK
    qv = q.reshape(B, L, d * N_GROUPS * D_B)
    kvv = kv.reshape(B, L, d * 2 * N_GROUPS * D_B)
    bias = _attn_bias(group, d)
    nq, nkv = N_GROUPS, 2 * N_GROUPS
    blk = (1, BLK, D_B)
    o, st = pl.pallas_call(
        _attn_kernel,
        grid=(B, d, nblk),
        in_specs=[
            pl.BlockSpec(blk, lambda b, r, n: (b, n, r * nq + group)),
            pl.BlockSpec(blk, lambda b, r, n: (b, jnp.maximum(n - 1, 0), r * nkv + group)),
            pl.BlockSpec(blk, lambda b, r, n: (b, n, r * nkv + group)),
            pl.BlockSpec(blk, lambda b, r, n: (b, jnp.maximum(n - 1, 0), r * nkv + N_GROUPS + group)),
            pl.BlockSpec(blk, lambda b, r, n: (b, n, r * nkv + N_GROUPS + group)),
            _const_spec(bias.shape),
        ],
        out_specs=[
            pl.BlockSpec(blk, lambda b, r, n: (b, n, r)),
            pl.BlockSpec((1, BLK, LANES), lambda b, r, n: (b, n, r)),
        ],
        out_shape=[
            jax.ShapeDtypeStruct((B, L, d * D_B), BF16),
            jax.ShapeDtypeStruct((B, L, d * LANES), F32),
        ],
        compiler_params=pltpu.CompilerParams(
            dimension_semantics=("arbitrary", "arbitrary", "arbitrary"),
            vmem_limit_bytes=VMEM_LIMIT_BYTES),
        name=f"attn_g{group}",
    )(qv, kvv, kvv, kvv, kvv, bias)
    return o.reshape(B, S, D_B), st.reshape(B, S, LANES)


def _head_expand_matrix():
    row = jnp.arange(LANES)[:, None]
    col = jnp.arange(D_B)[None, :]
    head = col // HEAD_DIM
    src = (head // 2) + (head % 2) * HEAD_DIM
    return (row == src).astype(BF16)


def _merge_kernel(o0_ref, o1_ref, o2_ref, s0_ref, s1_ref, s2_ref, gate_ref, h_ref,
                  e_ref, w_out_ref, g_ref, b_ref, out_ref):
    st = [s0_ref[0], s1_ref[0], s2_ref[0]]
    mx = jnp.maximum(jnp.maximum(st[0], st[1]), st[2])
    ex = [jnp.exp(s - mx) for s in st]
    inv = 1.0 / (ex[0] + ex[1] + ex[2])
    e = e_ref[...]
    merged = None
    for o_ref, x in zip((o0_ref, o1_ref, o2_ref), ex):
        w = x * inv
        w_hi = w.astype(BF16)
        w_lo = (w - w_hi.astype(F32)).astype(BF16)
        wexp = _dot(w_hi, e) + _dot(w_lo, e)
        term = wexp * o_ref[0].astype(F32)
        merged = term if merged is None else merged + term
    act = merged * _silu(gate_ref[0])
    y = _dot(act.astype(BF16), w_out_ref[...])
    out_ref[0] = _deepnorm_ln(h_ref[0], y, g_ref[...], b_ref[...])


def _merge_layer(outs, stats, gate, h, w_out, ln_g, ln_b, *, tile=512):
    B, S, D = h.shape
    e = _head_expand_matrix()
    row = lambda width: pl.BlockSpec((1, tile, width), lambda b, s: (b, s, 0))
    return pl.pallas_call(
        _merge_kernel,
        grid=(B, S // tile),
        in_specs=[row(D_B)] * 3 + [row(LANES)] * 3 + [row(D_B), row(D)] + [
            _const_spec(e.shape), _const_spec(w_out.shape),
            _const_spec(ln_g.shape), _const_spec(ln_b.shape)],
        out_specs=row(D),
        out_shape=jax.ShapeDtypeStruct((B, S, D), F32),
        compiler_params=pltpu.CompilerParams(
            dimension_semantics=("arbitrary", "arbitrary"),
            vmem_limit_bytes=VMEM_LIMIT_BYTES),
        name="merge_layer",
    )(*outs, *stats, gate, h, e, w_out, ln_g, ln_b)


def kernel(x, w_in_a, w_grp_a, scale_a, w_out_a, w_kv, w_in_b, w_out_b, ln_g, ln_b):
    n_q = N_GROUPS * D_B
    q_scale = jnp.concatenate([jnp.full((n_q,), HEAD_DIM ** -0.5, F32), jnp.ones((D_B,), F32)])
    h = x
    kv = None
    for l in range(DEPTH):
        g = ln_g[l][None, :]
        b = ln_b[l][None, :]
        if l < N_A_LAYERS:
            h = _pool_layer(h, w_in_a[l].astype(BF16), w_grp_a[l].astype(BF16),
                            scale_a[l][None, :], w_out_a[l].astype(BF16), g, b)
            if l == N_A_LAYERS - 1:
                kv = _project(h, w_kv.astype(BF16), n_bf16=w_kv.shape[1])[0]
        else:
            j = l - N_A_LAYERS
            q, gate = _project(h, (w_in_b[j] * q_scale).astype(BF16), n_bf16=n_q)
            outs, stats = zip(*[_attention_group(q, kv, grp) for grp in range(N_GROUPS)])
            h = _merge_layer(outs, stats, gate, h, w_out_b[j].astype(BF16), g, b)
    return h
```

```python
import functools

import jax
import jax.numpy as jnp
from jax import lax
from jax.experimental import pallas as pl
from jax.experimental.pallas import tpu as pltpu

D_MODEL = 1024
DEPTH = 4
N_A_LAYERS = DEPTH // 2
D_A = 2 * D_MODEL
POOL_WINDOWS = (2, 4, 8, 16)
N_POOL_GROUPS = len(POOL_WINDOWS)
POOL_GW = D_A // N_POOL_GROUPS
ATTN_PAIRS = ((128, 1), (512, 4), (2048, 16))
N_GROUPS = len(ATTN_PAIRS)
N_HEADS = 16
HEAD_DIM = D_MODEL // N_HEADS
D_B = N_HEADS * HEAD_DIM
BLK = 128
DN_ALPHA = (2 * DEPTH) ** 0.25
LN_EPS = 1e-5
NEG_INF = -1e30

LANES = 128
N_PAIRS = N_HEADS // 2
POOL_HALO = 16
VMEM_LIMIT_BYTES = 56 * 1024 * 1024

F32 = jnp.float32
BF16 = jnp.bfloat16


def _dot(a, b):
    return jnp.dot(a, b, preferred_element_type=F32)


def _silu(x):
    return x * (1.0 / (1.0 + jnp.exp(-x)))


def _deepnorm_ln(h, y, g, b):
    z = DN_ALPHA * h + y
    mu = jnp.mean(z, axis=-1, keepdims=True)
    zc = z - mu
    var = jnp.mean(zc * zc, axis=-1, keepdims=True)
    return zc * lax.rsqrt(var + LN_EPS) * g + b


def _const_spec(shape):
    nd = len(shape)
    return pl.BlockSpec(shape, lambda *_: (0,) * nd, pipeline_mode=pl.Buffered(1))


def _pool_layer_kernel(h_ref, w_in_ref, w_grp_ref, scale_ref, w_out_ref, g_ref, b_ref,
                       o_ref, uext_ref, *, tile):
    s = pl.program_id(1)

    @pl.when(s == 0)
    def _():
        uext_ref[0:POOL_HALO, :] = jnp.zeros((POOL_HALO, D_A), F32)

    h = h_ref[0]
    hb = h.astype(BF16)
    pos = s * tile + lax.broadcasted_iota(jnp.int32, (tile, 1), 0)
    y = jnp.zeros((tile, D_MODEL), F32)
    for g, w in enumerate(POOL_WINDOWS):
        c0 = g * POOL_GW
        u = _dot(hb, w_in_ref[:, c0:c0 + POOL_GW])
        uext_ref[POOL_HALO:POOL_HALO + tile, c0:c0 + POOL_GW] = u
        acc = u
        for k in range(1, w):
            acc = acc + uext_ref[POOL_HALO - k:POOL_HALO - k + tile, c0:c0 + POOL_GW]
        inv_cnt = 1.0 / jnp.minimum(pos + 1, w).astype(F32)
        pooled = acc * inv_cnt - u
        mixed = _dot(pooled.astype(BF16), w_grp_ref[g]) * scale_ref[:, c0:c0 + POOL_GW]
        gate = _dot(hb, w_in_ref[:, D_A + c0:D_A + c0 + POOL_GW])
        act = mixed * _silu(gate)
        y = y + _dot(act.astype(BF16), w_out_ref[c0:c0 + POOL_GW, :])
    uext_ref[0:POOL_HALO, :] = uext_ref[tile:tile + POOL_HALO, :]
    o_ref[0] = _deepnorm_ln(h, y, g_ref[...], b_ref[...])


def _pool_layer(h, w_in, w_grp, scale, w_out, ln_g, ln_b, *, tile=512):
    B, S, D = h.shape
    kern = functools.partial(_pool_layer_kernel, tile=tile)
    return pl.pallas_call(
        kern,
        grid=(B, S // tile),
        in_specs=[
            pl.BlockSpec((1, tile, D), lambda b, s: (b, s, 0)),
            _const_spec(w_in.shape),
            _const_spec(w_grp.shape),
            _const_spec(scale.shape),
            _const_spec(w_out.shape),
            _const_spec(ln_g.shape),
            _const_spec(ln_b.shape),
        ],
        out_specs=pl.BlockSpec((1, tile, D), lambda b, s: (b, s, 0)),
        out_shape=jax.ShapeDtypeStruct((B, S, D), F32),
        scratch_shapes=[pltpu.VMEM((POOL_HALO + tile, D_A), F32)],
        compiler_params=pltpu.CompilerParams(
            dimension_semantics=("arbitrary", "arbitrary"),
            vmem_limit_bytes=VMEM_LIMIT_BYTES),
        name="pool_layer",
    )(h, w_in, w_grp, scale, w_out, ln_g, ln_b)


def _proj_kernel(x_ref, w_ref, *refs, plan, tile):
    out_refs, slab_ref = refs[:len(plan)], refs[len(plan)]
    x = x_ref[0]
    lhs = {1: x.astype(BF16)}
    dils = sorted({d for _, d in plan if d > 1})
    if dils:
        for j in range(D_MODEL // LANES):
            slab_ref[j] = x[:, j * LANES:(j + 1) * LANES]
    for d in dils:
        planes = [
            jnp.concatenate([slab_ref[j, pl.ds(r, tile // d, stride=d), :]
                             for j in range(D_MODEL // LANES)], axis=1)
            for r in range(d)]
        lhs[d] = jnp.concatenate(planes, axis=0).astype(BF16)
    for (c0, d), o_ref in zip(plan, out_refs):
        res = _dot(lhs[d], w_ref[:, c0:c0 + D_B]).astype(o_ref.dtype)
        rows = tile // d
        for r in range(d):
            o_ref[0, r] = res[r * rows:(r + 1) * rows]


def _project(x, w, plan, dtypes, *, tile=512):
    B, S, D = x.shape
    out_shape = [jax.ShapeDtypeStruct((B, d, S // d, D_B), dt) for (_, d), dt in zip(plan, dtypes)]
    out_specs = [pl.BlockSpec((1, d, tile // d, D_B), lambda b, s: (b, 0, s, 0)) for _, d in plan]
    kern = functools.partial(_proj_kernel, plan=tuple(plan), tile=tile)
    return pl.pallas_call(
        kern,
        grid=(B, S // tile),
        in_specs=[pl.BlockSpec((1, tile, D), lambda b, s: (b, s, 0)), _const_spec(w.shape)],
        out_specs=out_specs,
        out_shape=out_shape,
        scratch_shapes=[pltpu.VMEM((D // LANES, tile, LANES), F32)],
        compiler_params=pltpu.CompilerParams(
            dimension_semantics=("arbitrary", "arbitrary"),
            vmem_limit_bytes=VMEM_LIMIT_BYTES),
        name="project",
    )(x, w)


def _attn_bias(group, dilation):
    n = N_GROUPS * N_HEADS
    i = jnp.arange(1, n + 1, dtype=F32)
    slopes = (2.0 ** (-8.0 * i / n)).reshape(N_GROUPS, N_HEADS)[group]
    a = jnp.arange(BLK)[:, None]
    c = jnp.arange(2 * BLK)[None, :]
    rel = BLK + a - c
    win_sub = ATTN_PAIRS[group][0] // dilation
    valid = (rel >= 0) & (rel <= win_sub)
    bias = -slopes[:, None, None] * (dilation * rel).astype(F32)[None]
    later = jnp.where(valid[None], bias, NEG_INF)
    first = jnp.where((valid & (c >= BLK))[None], bias, NEG_INF)
    return jnp.stack([later, first], axis=0)


def _attn_kernel(q_ref, kp_ref, kc_ref, vp_ref, vc_ref, bias_ref, o_ref, st_ref, *scratch, dilation):
    n = pl.program_id(1)
    r = pl.program_id(2)
    bsel = (n == 0).astype(jnp.int32)
    lane = lax.broadcasted_iota(jnp.int32, (1, LANES), 1)
    lo = lane < HEAD_DIM
    ones_lo = jnp.broadcast_to(jnp.where(lo, 1.0, 0.0).astype(BF16), (2 * BLK, LANES))
    ones_hi = jnp.broadcast_to(jnp.where(lo, 0.0, 1.0).astype(BF16), (2 * BLK, LANES))
    stats = jnp.zeros((BLK, LANES), F32)
    for p in range(N_PAIRS):
        cols = slice(p * LANES, (p + 1) * LANES)
        qp = q_ref[0, 0, :, cols]
        kp = jnp.concatenate([kp_ref[0, 0, :, cols], kc_ref[0, 0, :, cols]], axis=0)
        vp = jnp.concatenate([vp_ref[0, 0, :, cols], vc_ref[0, 0, :, cols]], axis=0)
        acc = jnp.zeros((BLK, 2 * LANES), F32)
        m_pair = jnp.zeros((BLK, LANES), F32)
        for hh in range(2):
            msk = lo if hh == 0 else jnp.logical_not(lo)
            qm = jnp.where(msk, qp, jnp.zeros_like(qp))
            sc = lax.dot_general(qm, kp, (((1,), (1,)), ((), ())), preferred_element_type=F32)
            sc = sc + bias_ref[bsel, 2 * p + hh]
            m = jnp.max(sc, axis=-1, keepdims=True)
            pe = jnp.exp(sc - m).astype(BF16)
            vm = jnp.where(msk, vp, jnp.zeros_like(vp))
            rhs = jnp.concatenate([vm, ones_lo if hh == 0 else ones_hi], axis=1)
            acc = acc + _dot(pe, rhs)
            m_pair = jnp.where(msk, m, m_pair)
        denom = acc[:, LANES:]
        o = acc[:, :LANES] / denom
        if dilation == 1:
            o_ref[0, :, cols] = o.astype(BF16)
        else:
            scratch[0][p, pl.ds(r, BLK, stride=dilation), :] = o
        lse = m_pair + jnp.log(denom)
        stats = jnp.where((lane & (HEAD_DIM - 1)) == p, lse, stats)
    if dilation == 1:
        st_ref[0] = stats
    else:
        st_ref[0, pl.ds(r, BLK, stride=dilation), :] = stats

        @pl.when(r == dilation - 1)
        def _():
            for p in range(N_PAIRS):
                o_ref[0, :, p * LANES:(p + 1) * LANES] = scratch[0][p].astype(BF16)


def _attention_group(q, k, v, group):
    _, d = ATTN_PAIRS[group]
    B, _, L, _ = q.shape
    S = L * d
    nblk = L // BLK
    bias = _attn_bias(group, d)
    blk = (1, 1, BLK, D_B)
    cur = lambda b, n, r: (b, r, n, 0)
    prev = lambda b, n, r: (b, r, jnp.maximum(n - 1, 0), 0)
    kern = functools.partial(_attn_kernel, dilation=d)
    return pl.pallas_call(
        kern,
        grid=(B, nblk, d),
        in_specs=[
            pl.BlockSpec(blk, cur),
            pl.BlockSpec(blk, prev),
            pl.BlockSpec(blk, cur),
            pl.BlockSpec(blk, prev),
            pl.BlockSpec(blk, cur),
            _const_spec(bias.shape),
        ],
        out_specs=[
            pl.BlockSpec((1, BLK * d, D_B), lambda b, n, r: (b, n, 0)),
            pl.BlockSpec((1, BLK * d, LANES), lambda b, n, r: (b, n, 0)),
        ],
        out_shape=[
            jax.ShapeDtypeStruct((B, S, D_B), BF16),
            jax.ShapeDtypeStruct((B, S, LANES), F32),
        ],
        scratch_shapes=[pltpu.VMEM((N_PAIRS, BLK * d, LANES), F32)] if d > 1 else [],
        compiler_params=pltpu.CompilerParams(
            dimension_semantics=("arbitrary", "arbitrary", "arbitrary"),
            vmem_limit_bytes=VMEM_LIMIT_BYTES),
        name=f"attn_g{group}",
    )(q, k, k, v, v, bias)


def _head_expand_matrix():
    row = jnp.arange(LANES)[:, None]
    col = jnp.arange(D_B)[None, :]
    head = col // HEAD_DIM
    src = (head // 2) + (head % 2) * HEAD_DIM
    return (row == src).astype(BF16)


def _merge_kernel(o0_ref, o1_ref, o2_ref, s0_ref, s1_ref, s2_ref, gate_ref, h_ref,
                  e_ref, w_out_ref, g_ref, b_ref, out_ref):
    st = [s0_ref[0], s1_ref[0], s2_ref[0]]
    mx = jnp.maximum(jnp.maximum(st[0], st[1]), st[2])
    ex = [jnp.exp(s - mx) for s in st]
    inv = 1.0 / (ex[0] + ex[1] + ex[2])
    e = e_ref[...]
    merged = None
    for o_ref, x in zip((o0_ref, o1_ref, o2_ref), ex):
        w = x * inv
        w_hi = w.astype(BF16)
        w_lo = (w - w_hi.astype(F32)).astype(BF16)
        wexp = _dot(w_hi, e) + _dot(w_lo, e)
        term = wexp * o_ref[0].astype(F32)
        merged = term if merged is None else merged + term
    act = merged * _silu(gate_ref[0])
    y = _dot(act.astype(BF16), w_out_ref[...])
    out_ref[0] = _deepnorm_ln(h_ref[0], y, g_ref[...], b_ref[...])


def _merge_layer(outs, stats, gate, h, w_out, ln_g, ln_b, *, tile=512):
    B, S, D = h.shape
    e = _head_expand_matrix()
    row = lambda width: pl.BlockSpec((1, tile, width), lambda b, s: (b, s, 0))
    return pl.pallas_call(
        _merge_kernel,
        grid=(B, S // tile),
        in_specs=[row(D_B)] * 3 + [row(LANES)] * 3 + [row(D_B), row(D)] + [
            _const_spec(e.shape), _const_spec(w_out.shape),
            _const_spec(ln_g.shape), _const_spec(ln_b.shape)],
        out_specs=row(D),
        out_shape=jax.ShapeDtypeStruct((B, S, D), F32),
        compiler_params=pltpu.CompilerParams(
            dimension_semantics=("arbitrary", "arbitrary"),
            vmem_limit_bytes=VMEM_LIMIT_BYTES),
        name="merge_layer",
    )(*outs, *stats, gate, h, e, w_out, ln_g, ln_b)


def kernel(x, w_in_a, w_grp_a, scale_a, w_out_a, w_kv, w_in_b, w_out_b, ln_g, ln_b):
    n_q = N_GROUPS * D_B
    B, S, _ = x.shape
    dils = [d for _, d in ATTN_PAIRS]
    q_scale = jnp.concatenate([jnp.full((n_q,), HEAD_DIM ** -0.5, F32), jnp.ones((D_B,), F32)])
    kv_plan = ([(grp * D_B, d) for grp, d in enumerate(dils)]
               + [(n_q + grp * D_B, d) for grp, d in enumerate(dils)])
    q_plan = [(grp * D_B, d) for grp, d in enumerate(dils)] + [(n_q, 1)]
    h = x
    ks = vs = None
    for l in range(DEPTH):
        g = ln_g[l][None, :]
        b = ln_b[l][None, :]
        if l < N_A_LAYERS:
            h = _pool_layer(h, w_in_a[l].astype(BF16), w_grp_a[l].astype(BF16),
                            scale_a[l][None, :], w_out_a[l].astype(BF16), g, b)
            if l == N_A_LAYERS - 1:
                kv = _project(h, w_kv.astype(BF16), kv_plan, [BF16] * (2 * N_GROUPS))
                ks, vs = kv[:N_GROUPS], kv[N_GROUPS:]
        else:
            j = l - N_A_LAYERS
            *qs, gate = _project(h, (w_in_b[j] * q_scale).astype(BF16), q_plan,
                                 [BF16] * N_GROUPS + [F32])
            outs, stats = zip(*[_attention_group(qs[grp], ks[grp], vs[grp], grp)
                                for grp in range(N_GROUPS)])
            h = _merge_layer(outs, stats, gate.reshape(B, S, D_B), h, w_out_b[j].astype(BF16), g, b)
    return h
```

```python
import functools

import jax
import jax.numpy as jnp
from jax import lax
from jax.experimental import pallas as pl
from jax.experimental.pallas import tpu as pltpu

D_MODEL = 1024
DEPTH = 4
N_A_LAYERS = DEPTH // 2
D_A = 2 * D_MODEL
POOL_WINDOWS = (2, 4, 8, 16)
N_POOL_GROUPS = len(POOL_WINDOWS)
POOL_GW = D_A // N_POOL_GROUPS
ATTN_PAIRS = ((128, 1), (512, 4), (2048, 16))
N_GROUPS = len(ATTN_PAIRS)
N_HEADS = 16
HEAD_DIM = D_MODEL // N_HEADS
D_B = N_HEADS * HEAD_DIM
BLK = 128
DN_ALPHA = (2 * DEPTH) ** 0.25
LN_EPS = 1e-5
NEG_INF = -1e30

LANES = 128
N_PAIRS = N_HEADS // 2
SUBLANES = 8
POOL_HALO = SUBLANES * N_POOL_GROUPS
VMEM_LIMIT_BYTES = 56 * 1024 * 1024

F32 = jnp.float32
BF16 = jnp.bfloat16


def _dot(a, b):
    return jnp.dot(a, b, preferred_element_type=F32)


def _silu(x):
    return x * (1.0 / (1.0 + jnp.exp(-x)))


def _deepnorm_ln(h, y, g, b):
    z = DN_ALPHA * h + y
    mu = jnp.mean(z, axis=-1, keepdims=True)
    zc = z - mu
    var = jnp.mean(zc * zc, axis=-1, keepdims=True)
    return zc * lax.rsqrt(var + LN_EPS) * g + b


def _const_spec(shape):
    nd = len(shape)
    return pl.BlockSpec(shape, lambda *_: (0,) * nd, pipeline_mode=pl.Buffered(1))


def _pool_layer_kernel(h_ref, w_in_ref, w_grp_ref, scale_ref, w_out_ref, g_ref, b_ref,
                       o_ref, hext_ref, *sum_refs, tile):
    s = pl.program_id(1)
    halo, end = POOL_HALO, POOL_HALO + tile

    @pl.when(s == 0)
    def _():
        hext_ref[0:halo, :] = jnp.zeros((halo, D_MODEL), F32)

    h = h_ref[0]
    hb = h.astype(BF16)
    hext_ref[halo:end, :] = h
    src, w, lo = hext_ref, 1, 0
    window_sums = []
    for dst in sum_refs:
        lo += SUBLANES
        dst[lo:end, :] = src[lo:end, :] + src[lo - w:end - w, :]
        window_sums.append(dst[halo:end, :])
        src, w = dst, 2 * w
    window_sums.append(src[halo:end, :] + src[halo - w:end - w, :])

    pos = s * tile + lax.broadcasted_iota(jnp.int32, (tile, 1), 0)
    y = jnp.zeros((tile, D_MODEL), F32)
    for g, (w, wsum) in enumerate(zip(POOL_WINDOWS, window_sums)):
        c0 = g * POOL_GW
        inv_cnt = 1.0 / jnp.minimum(pos + 1, w).astype(F32)
        lhs = (wsum * inv_cnt - h).astype(BF16)
        pooled = _dot(lhs, w_in_ref[:, c0:c0 + POOL_GW])
        mixed = _dot(pooled.astype(BF16), w_grp_ref[g]) * scale_ref[:, c0:c0 + POOL_GW]
        gate = _dot(hb, w_in_ref[:, D_A + c0:D_A + c0 + POOL_GW])
        act = mixed * _silu(gate)
        y = y + _dot(act.astype(BF16), w_out_ref[c0:c0 + POOL_GW, :])
    hext_ref[0:halo, :] = hext_ref[tile:end, :]
    o_ref[0] = _deepnorm_ln(h, y, g_ref[...], b_ref[...])


def _pool_layer(h, w_in, w_grp, scale, w_out, ln_g, ln_b, *, tile=512):
    B, S, D = h.shape
    kern = functools.partial(_pool_layer_kernel, tile=tile)
    return pl.pallas_call(
        kern,
        grid=(B, S // tile),
        in_specs=[
            pl.BlockSpec((1, tile, D), lambda b, s: (b, s, 0)),
            _const_spec(w_in.shape),
            _const_spec(w_grp.shape),
            _const_spec(scale.shape),
            _const_spec(w_out.shape),
            _const_spec(ln_g.shape),
            _const_spec(ln_b.shape),
        ],
        out_specs=pl.BlockSpec((1, tile, D), lambda b, s: (b, s, 0)),
        out_shape=jax.ShapeDtypeStruct((B, S, D), F32),
        scratch_shapes=[pltpu.VMEM((POOL_HALO + tile, D), F32)] * N_POOL_GROUPS,
        compiler_params=pltpu.CompilerParams(
            dimension_semantics=("arbitrary", "arbitrary"),
            vmem_limit_bytes=VMEM_LIMIT_BYTES),
        name="pool_layer",
    )(h, w_in, w_grp, scale, w_out, ln_g, ln_b)


def _proj_kernel(x_ref, w_ref, *refs, plan, tile):
    out_refs, slab_ref = refs[:len(plan)], refs[len(plan)]
    x = x_ref[0]
    lhs = {1: x.astype(BF16)}
    dils = sorted({d for _, d in plan if d > 1})
    if dils:
        for j in range(D_MODEL // LANES):
            slab_ref[j] = x[:, j * LANES:(j + 1) * LANES]
    for d in dils:
        planes = [
            jnp.concatenate([slab_ref[j, pl.ds(r, tile // d, stride=d), :]
                             for j in range(D_MODEL // LANES)], axis=1)
            for r in range(d)]
        lhs[d] = jnp.concatenate(planes, axis=0).astype(BF16)
    for (c0, d), o_ref in zip(plan, out_refs):
        res = _dot(lhs[d], w_ref[:, c0:c0 + D_B]).astype(o_ref.dtype)
        rows = tile // d
        for r in range(d):
            o_ref[0, r] = res[r * rows:(r + 1) * rows]


def _project(x, w, plan, dtypes, *, tile=512):
    B, S, D = x.shape
    out_shape = [jax.ShapeDtypeStruct((B, d, S // d, D_B), dt) for (_, d), dt in zip(plan, dtypes)]
    out_specs = [pl.BlockSpec((1, d, tile // d, D_B), lambda b, s: (b, 0, s, 0)) for _, d in plan]
    kern = functools.partial(_proj_kernel, plan=tuple(plan), tile=tile)
    return pl.pallas_call(
        kern,
        grid=(B, S // tile),
        in_specs=[pl.BlockSpec((1, tile, D), lambda b, s: (b, s, 0)), _const_spec(w.shape)],
        out_specs=out_specs,
        out_shape=out_shape,
        scratch_shapes=[pltpu.VMEM((D // LANES, tile, LANES), F32)],
        compiler_params=pltpu.CompilerParams(
            dimension_semantics=("arbitrary", "arbitrary"),
            vmem_limit_bytes=VMEM_LIMIT_BYTES),
        name="project",
    )(x, w)


def _attn_bias(group, dilation):
    n = N_GROUPS * N_HEADS
    i = jnp.arange(1, n + 1, dtype=F32)
    slopes = (2.0 ** (-8.0 * i / n)).reshape(N_GROUPS, N_HEADS)[group]
    a = jnp.arange(BLK)[:, None]
    c = jnp.arange(2 * BLK)[None, :]
    rel = BLK + a - c
    win_sub = ATTN_PAIRS[group][0] // dilation
    valid = (rel >= 0) & (rel <= win_sub)
    bias = -slopes[:, None, None] * (dilation * rel).astype(F32)[None]
    later = jnp.where(valid[None], bias, NEG_INF)
    first = jnp.where((valid & (c >= BLK))[None], bias, NEG_INF)
    return jnp.stack([later, first], axis=0)


def _attn_kernel(q_ref, kp_ref, kc_ref, vp_ref, vc_ref, bias_ref, o_ref, st_ref, *scratch,
                 dilation, res_per_step, blk_per_step):
    n = pl.program_id(1)
    rg = pl.program_id(2)
    lane = lax.broadcasted_iota(jnp.int32, (1, LANES), 1)
    lo = lane < HEAD_DIM
    ones_lo = jnp.broadcast_to(jnp.where(lo, 1.0, 0.0).astype(BF16), (2 * BLK, LANES))
    ones_hi = jnp.broadcast_to(jnp.where(lo, 0.0, 1.0).astype(BF16), (2 * BLK, LANES))

    def block(rr, j, key_window, bsel):
        row0 = j * BLK if isinstance(j, int) else pl.multiple_of(j * BLK, BLK)
        out0 = row0 * dilation + rg * res_per_step + rr
        stats = jnp.zeros((BLK, LANES), F32)
        for p in range(N_PAIRS):
            cols = slice(p * LANES, (p + 1) * LANES)
            qp = q_ref[0, rr, pl.ds(row0, BLK), cols]
            kp = key_window(kp_ref, kc_ref, cols)
            vp = key_window(vp_ref, vc_ref, cols)
            acc = jnp.zeros((BLK, 2 * LANES), F32)
            m_pair = jnp.zeros((BLK, LANES), F32)
            for hh in range(2):
                msk = lo if hh == 0 else jnp.logical_not(lo)
                qm = jnp.where(msk, qp, jnp.zeros_like(qp))
                sc = lax.dot_general(qm, kp, (((1,), (1,)), ((), ())), preferred_element_type=F32)
                sc = sc + bias_ref[bsel, 2 * p + hh]
                m = jnp.max(sc, axis=-1, keepdims=True)
                pe = jnp.exp(sc - m).astype(BF16)
                vm = jnp.where(msk, vp, jnp.zeros_like(vp))
                rhs = jnp.concatenate([vm, ones_lo if hh == 0 else ones_hi], axis=1)
                acc = acc + _dot(pe, rhs)
                m_pair = jnp.where(msk, m, m_pair)
            denom = acc[:, LANES:]
            o = acc[:, :LANES] / denom
            if dilation == 1:
                o_ref[0, pl.ds(row0, BLK), cols] = o.astype(BF16)
            else:
                scratch[0][p, pl.ds(out0, BLK, stride=dilation), :] = o
            lse = m_pair + jnp.log(denom)
            stats = jnp.where((lane & (HEAD_DIM - 1)) == p, lse, stats)
        if dilation == 1:
            st_ref[0, pl.ds(row0, BLK), :] = stats
        else:
            st_ref[0, pl.ds(out0, BLK, stride=dilation), :] = stats

    first_tile = (n == 0).astype(jnp.int32)

    def first_block(rr, carry):
        window = lambda prev, cur, cols: jnp.concatenate(
            [prev[0, rr, :, cols], cur[0, rr, 0:BLK, cols]], axis=0)
        block(rr, 0, window, first_tile)
        return carry

    lax.fori_loop(0, res_per_step, first_block, 0)

    rest = blk_per_step - 1
    if rest:
        def later_block(i, carry):
            rr, j = i // rest, 1 + i % rest
            k0 = pl.multiple_of((j - 1) * BLK, BLK)
            window = lambda prev, cur, cols: cur[0, rr, pl.ds(k0, 2 * BLK), cols]
            block(rr, j, window, 0)
            return carry

        lax.fori_loop(0, res_per_step * rest, later_block, 0)

    if dilation > 1:
        @pl.when(rg == dilation // res_per_step - 1)
        def _():
            for p in range(N_PAIRS):
                o_ref[0, :, p * LANES:(p + 1) * LANES] = scratch[0][p].astype(BF16)


ATTN_STEP_SHAPE = {1: (1, 8), 4: (4, 2), 16: (8, 1)}


def _attention_group(q, k, v, group):
    _, d = ATTN_PAIRS[group]
    B, _, L, _ = q.shape
    S = L * d
    rb, qb = ATTN_STEP_SHAPE[d]
    rows = qb * BLK
    bias = _attn_bias(group, d)
    cur_spec = pl.BlockSpec((1, rb, rows, D_B), lambda b, n, rg: (b, rg, n, 0))
    prev_spec = pl.BlockSpec((1, rb, BLK, D_B), lambda b, n, rg: (b, rg, jnp.maximum(n * qb - 1, 0), 0))
    kern = functools.partial(_attn_kernel, dilation=d, res_per_step=rb, blk_per_step=qb)
    return pl.pallas_call(
        kern,
        grid=(B, L // rows, d // rb),
        in_specs=[cur_spec, prev_spec, cur_spec, prev_spec, cur_spec, _const_spec(bias.shape)],
        out_specs=[
            pl.BlockSpec((1, rows * d, D_B), lambda b, n, rg: (b, n, 0)),
            pl.BlockSpec((1, rows * d, LANES), lambda b, n, rg: (b, n, 0)),
        ],
        out_shape=[
            jax.ShapeDtypeStruct((B, S, D_B), BF16),
            jax.ShapeDtypeStruct((B, S, LANES), F32),
        ],
        scratch_shapes=[pltpu.VMEM((N_PAIRS, rows * d, LANES), F32)] if d > 1 else [],
        compiler_params=pltpu.CompilerParams(
            dimension_semantics=("arbitrary", "arbitrary", "arbitrary"),
            vmem_limit_bytes=VMEM_LIMIT_BYTES),
        name=f"attn_g{group}",
    )(q, k, k, v, v, bias)


def _head_expand_matrix():
    row = jnp.arange(LANES)[:, None]
    col = jnp.arange(D_B)[None, :]
    head = col // HEAD_DIM
    src = (head // 2) + (head % 2) * HEAD_DIM
    return (row == src).astype(BF16)


def _merge_kernel(o0_ref, o1_ref, o2_ref, s0_ref, s1_ref, s2_ref, gate_ref, h_ref,
                  e_ref, w_out_ref, g_ref, b_ref, out_ref):
    st = [s0_ref[0], s1_ref[0], s2_ref[0]]
    mx = jnp.maximum(jnp.maximum(st[0], st[1]), st[2])
    ex = [jnp.exp(s - mx) for s in st]
    inv = 1.0 / (ex[0] + ex[1] + ex[2])
    e = e_ref[...]
    merged = None
    for o_ref, x in zip((o0_ref, o1_ref, o2_ref), ex):
        w = x * inv
        w_hi = w.astype(BF16)
        w_lo = (w - w_hi.astype(F32)).astype(BF16)
        wexp = _dot(w_hi, e) + _dot(w_lo, e)
        term = wexp * o_ref[0].astype(F32)
        merged = term if merged is None else merged + term
    act = merged * _silu(gate_ref[0])
    y = _dot(act.astype(BF16), w_out_ref[...])
    out_ref[0] = _deepnorm_ln(h_ref[0], y, g_ref[...], b_ref[...])


def _merge_layer(outs, stats, gate, h, w_out, ln_g, ln_b, *, tile=512):
    B, S, D = h.shape
    e = _head_expand_matrix()
    row = lambda width: pl.BlockSpec((1, tile, width), lambda b, s: (b, s, 0))
    return pl.pallas_call(
        _merge_kernel,
        grid=(B, S // tile),
        in_specs=[row(D_B)] * 3 + [row(LANES)] * 3 + [row(D_B), row(D)] + [
            _const_spec(e.shape), _const_spec(w_out.shape),
            _const_spec(ln_g.shape), _const_spec(ln_b.shape)],
        out_specs=row(D),
        out_shape=jax.ShapeDtypeStruct((B, S, D), F32),
        compiler_params=pltpu.CompilerParams(
            dimension_semantics=("arbitrary", "arbitrary"),
            vmem_limit_bytes=VMEM_LIMIT_BYTES),
        name="merge_layer",
    )(*outs, *stats, gate, h, e, w_out, ln_g, ln_b)


def kernel(x, w_in_a, w_grp_a, scale_a, w_out_a, w_kv, w_in_b, w_out_b, ln_g, ln_b):
    n_q = N_GROUPS * D_B
    B, S, _ = x.shape
    dils = [d for _, d in ATTN_PAIRS]
    q_scale = jnp.concatenate([jnp.full((n_q,), HEAD_DIM ** -0.5, F32), jnp.ones((D_B,), F32)])
    kv_plan = ([(grp * D_B, d) for grp, d in enumerate(dils)]
               + [(n_q + grp * D_B, d) for grp, d in enumerate(dils)])
    q_plan = [(grp * D_B, d) for grp, d in enumerate(dils)] + [(n_q, 1)]
    h = x
    ks = vs = None
    for l in range(DEPTH):
        g = ln_g[l][None, :]
        b = ln_b[l][None, :]
        if l < N_A_LAYERS:
            h = _pool_layer(h, w_in_a[l].astype(BF16), w_grp_a[l].astype(BF16),
                            scale_a[l][None, :], w_out_a[l].astype(BF16), g, b)
            if l == N_A_LAYERS - 1:
                kv = _project(h, w_kv.astype(BF16), kv_plan, [BF16] * (2 * N_GROUPS))
                ks, vs = kv[:N_GROUPS], kv[N_GROUPS:]
        else:
            j = l - N_A_LAYERS
            *qs, gate = _project(h, (w_in_b[j] * q_scale).astype(BF16), q_plan,
                                 [BF16] * N_GROUPS + [F32])
            outs, stats = zip(*[_attention_group(qs[grp], ks[grp], vs[grp], grp)
                                for grp in range(N_GROUPS)])
            h = _merge_layer(outs, stats, gate.reshape(B, S, D_B), h, w_out_b[j].astype(BF16), g, b)
    return h
```

```python
import functools

import jax
import jax.numpy as jnp
from jax import lax
from jax.experimental import pallas as pl
from jax.experimental.pallas import tpu as pltpu

D_MODEL = 1024
DEPTH = 4
N_A_LAYERS = DEPTH // 2
D_A = 2 * D_MODEL
POOL_WINDOWS = (2, 4, 8, 16)
N_POOL_GROUPS = len(POOL_WINDOWS)
POOL_GW = D_A // N_POOL_GROUPS
ATTN_PAIRS = ((128, 1), (512, 4), (2048, 16))
N_GROUPS = len(ATTN_PAIRS)
N_HEADS = 16
HEAD_DIM = D_MODEL // N_HEADS
D_B = N_HEADS * HEAD_DIM
BLK = 128
DN_ALPHA = (2 * DEPTH) ** 0.25
LN_EPS = 1e-5
NEG_INF = -1e30
LOG2_E = 1.4426950408889634

LANES = 128
N_PAIRS = N_HEADS // 2
SUBLANES = 8
POOL_HALO = SUBLANES * N_POOL_GROUPS
VMEM_LIMIT_BYTES = 56 * 1024 * 1024

F32 = jnp.float32
BF16 = jnp.bfloat16


def _dot(a, b):
    return jnp.dot(a, b, preferred_element_type=F32)


def _silu(x):
    hx = 0.5 * x
    return hx + hx * jnp.tanh(hx)


def _deepnorm_ln(h, y, g, b):
    z = DN_ALPHA * h + y
    mu = jnp.mean(z, axis=-1, keepdims=True)
    zc = z - mu
    var = jnp.mean(zc * zc, axis=-1, keepdims=True)
    return zc * lax.rsqrt(var + LN_EPS) * g + b


def _const_spec(shape):
    nd = len(shape)
    return pl.BlockSpec(shape, lambda *_: (0,) * nd, pipeline_mode=pl.Buffered(1))


def _pool_layer_kernel(h_ref, w_in_ref, w_grp_ref, scale_ref, w_out_ref, g_ref, b_ref,
                       o_ref, hext_ref, *sum_refs, tile):
    s = pl.program_id(1)
    halo, end = POOL_HALO, POOL_HALO + tile

    @pl.when(s == 0)
    def _():
        hext_ref[0:halo, :] = jnp.zeros((halo, D_MODEL), F32)

    h = h_ref[0]
    hb = h.astype(BF16)
    hext_ref[halo:end, :] = h
    src, w, lo = hext_ref, 1, 0
    window_sums = []
    for dst in sum_refs:
        lo += SUBLANES
        dst[lo:end, :] = src[lo:end, :] + src[lo - w:end - w, :]
        window_sums.append(dst[halo:end, :])
        src, w = dst, 2 * w
    window_sums.append(src[halo:end, :] + src[halo - w:end - w, :])

    pos = s * tile + lax.broadcasted_iota(jnp.int32, (tile, 1), 0)
    y = jnp.zeros((tile, D_MODEL), F32)
    for g, (w, wsum) in enumerate(zip(POOL_WINDOWS, window_sums)):
        c0 = g * POOL_GW
        inv_cnt = 1.0 / jnp.minimum(pos + 1, w).astype(F32)
        lhs = (wsum * inv_cnt - h).astype(BF16)
        pooled = _dot(lhs, w_in_ref[:, c0:c0 + POOL_GW])
        mixed = _dot(pooled.astype(BF16), w_grp_ref[g]) * scale_ref[:, c0:c0 + POOL_GW]
        gate = _dot(hb, w_in_ref[:, D_A + c0:D_A + c0 + POOL_GW])
        act = mixed * _silu(gate)
        y = y + _dot(act.astype(BF16), w_out_ref[c0:c0 + POOL_GW, :])
    hext_ref[0:halo, :] = hext_ref[tile:end, :]
    o_ref[0] = _deepnorm_ln(h, y, g_ref[...], b_ref[...])


def _pool_layer(h, w_in, w_grp, scale, w_out, ln_g, ln_b, *, tile=512):
    B, S, D = h.shape
    kern = functools.partial(_pool_layer_kernel, tile=tile)
    return pl.pallas_call(
        kern,
        grid=(B, S // tile),
        in_specs=[
            pl.BlockSpec((1, tile, D), lambda b, s: (b, s, 0)),
            _const_spec(w_in.shape),
            _const_spec(w_grp.shape),
            _const_spec(scale.shape),
            _const_spec(w_out.shape),
            _const_spec(ln_g.shape),
            _const_spec(ln_b.shape),
        ],
        out_specs=pl.BlockSpec((1, tile, D), lambda b, s: (b, s, 0)),
        out_shape=jax.ShapeDtypeStruct((B, S, D), F32),
        scratch_shapes=[pltpu.VMEM((POOL_HALO + tile, D), F32)] * N_POOL_GROUPS,
        compiler_params=pltpu.CompilerParams(
            dimension_semantics=("arbitrary", "arbitrary"),
            vmem_limit_bytes=VMEM_LIMIT_BYTES),
        name="pool_layer",
    )(h, w_in, w_grp, scale, w_out, ln_g, ln_b)


def _proj_kernel(x_ref, w_ref, *refs, plan, tile):
    out_refs, slab_ref = refs[:len(plan)], refs[len(plan)]
    x = x_ref[0]
    lhs = {1: x.astype(BF16)}
    dils = sorted({d for _, d in plan if d > 1})
    if dils:
        for j in range(D_MODEL // LANES):
            slab_ref[j] = x[:, j * LANES:(j + 1) * LANES]
    for d in dils:
        planes = [
            jnp.concatenate([slab_ref[j, pl.ds(r, tile // d, stride=d), :]
                             for j in range(D_MODEL // LANES)], axis=1)
            for r in range(d)]
        lhs[d] = jnp.concatenate(planes, axis=0).astype(BF16)
    for (c0, d), o_ref in zip(plan, out_refs):
        res = _dot(lhs[d], w_ref[:, c0:c0 + D_B]).astype(o_ref.dtype)
        rows = tile // d
        for r in range(d):
            o_ref[0, r] = res[r * rows:(r + 1) * rows]


def _project(x, w, plan, dtypes, *, tile=512):
    B, S, D = x.shape
    out_shape = [jax.ShapeDtypeStruct((B, d, S // d, D_B), dt) for (_, d), dt in zip(plan, dtypes)]
    out_specs = [pl.BlockSpec((1, d, tile // d, D_B), lambda b, s: (b, 0, s, 0)) for _, d in plan]
    kern = functools.partial(_proj_kernel, plan=tuple(plan), tile=tile)
    return pl.pallas_call(
        kern,
        grid=(B, S // tile),
        in_specs=[pl.BlockSpec((1, tile, D), lambda b, s: (b, s, 0)), _const_spec(w.shape)],
        out_specs=out_specs,
        out_shape=out_shape,
        scratch_shapes=[pltpu.VMEM((D // LANES, tile, LANES), F32)],
        compiler_params=pltpu.CompilerParams(
            dimension_semantics=("arbitrary", "arbitrary"),
            vmem_limit_bytes=VMEM_LIMIT_BYTES),
        name="project",
    )(x, w)


def _attn_bias(group, dilation):
    n = N_GROUPS * N_HEADS
    i = jnp.arange(1, n + 1, dtype=F32)
    slopes = (2.0 ** (-8.0 * i / n)).reshape(N_GROUPS, N_HEADS)[group]
    a = jnp.arange(BLK)[:, None]
    c = jnp.arange(2 * BLK)[None, :]
    rel = BLK + a - c
    win_sub = ATTN_PAIRS[group][0] // dilation
    valid = (rel >= 0) & (rel <= win_sub)
    bias = -slopes[:, None, None] * (dilation * rel).astype(F32)[None] * LOG2_E
    later = jnp.where(valid[None], bias, NEG_INF)
    first = jnp.where((valid & (c >= BLK))[None], bias, NEG_INF)
    return jnp.stack([later, first], axis=0).reshape(2, N_PAIRS, 2 * BLK, 2 * BLK)


def _attn_kernel(q_ref, kp_ref, kc_ref, vp_ref, vc_ref, bias_ref, o_ref, st_ref, *scratch,
                 dilation, res_per_step, blk_per_step):
    n = pl.program_id(1)
    rg = pl.program_id(2)
    lane = lax.broadcasted_iota(jnp.int32, (1, LANES), 1)
    lo = lane < HEAD_DIM
    ones = jnp.ones((2 * BLK, LANES), BF16)

    def block(rr, j, key_window, bsel):
        row0 = j * BLK if isinstance(j, int) else pl.multiple_of(j * BLK, BLK)
        out0 = row0 * dilation + rg * res_per_step + rr
        stats = jnp.zeros((BLK, LANES), F32)
        for p in range(N_PAIRS):
            cols = slice(p * LANES, (p + 1) * LANES)
            qp = q_ref[0, rr, pl.ds(row0, BLK), cols]
            kp = key_window(kp_ref, kc_ref, cols)
            vp = key_window(vp_ref, vc_ref, cols)
            zero = jnp.zeros_like(qp)
            q2 = jnp.concatenate([jnp.where(lo, qp, zero), jnp.where(lo, zero, qp)], axis=0)
            sc = lax.dot_general(q2, kp, (((1,), (1,)), ((), ())), preferred_element_type=F32)
            sc = sc + bias_ref[bsel, p]
            m = jnp.max(sc, axis=-1, keepdims=True)
            pe = jnp.exp2((sc - m).astype(BF16))
            acc = _dot(pe, jnp.concatenate([vp, ones], axis=1))
            pick = lambda x: jnp.where(lo, x[:BLK], x[BLK:])
            denom = pick(acc[:, LANES:])
            m_pair = pick(m)
            o = pick(acc[:, :LANES]) / denom
            if dilation == 1:
                o_ref[0, pl.ds(row0, BLK), cols] = o.astype(BF16)
            else:
                scratch[0][p, pl.ds(out0, BLK, stride=dilation), :] = o
            lse = m_pair + jnp.log2(denom)
            stats = jnp.where((lane & (HEAD_DIM - 1)) == p, lse, stats)
        if dilation == 1:
            st_ref[0, pl.ds(row0, BLK), :] = stats
        else:
            st_ref[0, pl.ds(out0, BLK, stride=dilation), :] = stats

    first_tile = (n == 0).astype(jnp.int32)

    def first_block(rr, carry):
        window = lambda prev, cur, cols: jnp.concatenate(
            [prev[0, rr, :, cols], cur[0, rr, 0:BLK, cols]], axis=0)
        block(rr, 0, window, first_tile)
        return carry

    lax.fori_loop(0, res_per_step, first_block, 0)

    rest = blk_per_step - 1
    if rest:
        def later_block(i, carry):
            rr, j = i // rest, 1 + i % rest
            k0 = pl.multiple_of((j - 1) * BLK, BLK)
            window = lambda prev, cur, cols: cur[0, rr, pl.ds(k0, 2 * BLK), cols]
            block(rr, j, window, 0)
            return carry

        lax.fori_loop(0, res_per_step * rest, later_block, 0)

    if dilation > 1:
        @pl.when(rg == dilation // res_per_step - 1)
        def _():
            for p in range(N_PAIRS):
                o_ref[0, :, p * LANES:(p + 1) * LANES] = scratch[0][p].astype(BF16)


ATTN_STEP_SHAPE = {1: (1, 8), 4: (4, 2), 16: (8, 1)}


def _attention_group(q, k, v, group):
    _, d = ATTN_PAIRS[group]
    B, _, L, _ = q.shape
    S = L * d
    rb, qb = ATTN_STEP_SHAPE[d]
    rows = qb * BLK
    bias = _attn_bias(group, d)
    cur_spec = pl.BlockSpec((1, rb, rows, D_B), lambda b, n, rg: (b, rg, n, 0))
    prev_spec = pl.BlockSpec((1, rb, BLK, D_B), lambda b, n, rg: (b, rg, jnp.maximum(n * qb - 1, 0), 0))
    kern = functools.partial(_attn_kernel, dilation=d, res_per_step=rb, blk_per_step=qb)
    return pl.pallas_call(
        kern,
        grid=(B, L // rows, d // rb),
        in_specs=[cur_spec, prev_spec, cur_spec, prev_spec, cur_spec, _const_spec(bias.shape)],
        out_specs=[
            pl.BlockSpec((1, rows * d, D_B), lambda b, n, rg: (b, n, 0)),
            pl.BlockSpec((1, rows * d, LANES), lambda b, n, rg: (b, n, 0)),
        ],
        out_shape=[
            jax.ShapeDtypeStruct((B, S, D_B), BF16),
            jax.ShapeDtypeStruct((B, S, LANES), F32),
        ],
        scratch_shapes=[pltpu.VMEM((N_PAIRS, rows * d, LANES), F32)] if d > 1 else [],
        compiler_params=pltpu.CompilerParams(
            dimension_semantics=("arbitrary", "arbitrary", "arbitrary"),
            vmem_limit_bytes=VMEM_LIMIT_BYTES),
        name=f"attn_g{group}",
    )(q, k, k, v, v, bias)


def _head_expand_matrix():
    row = jnp.arange(LANES)[:, None]
    col = jnp.arange(D_B)[None, :]
    head = col // HEAD_DIM
    src = (head // 2) + (head % 2) * HEAD_DIM
    return (row == src).astype(BF16)


def _merge_kernel(o0_ref, o1_ref, o2_ref, s0_ref, s1_ref, s2_ref, gate_ref, h_ref,
                  e_ref, w_out_ref, g_ref, b_ref, out_ref):
    st = [s0_ref[0], s1_ref[0], s2_ref[0]]
    mx = jnp.maximum(jnp.maximum(st[0], st[1]), st[2])
    ex = [jnp.exp2(s - mx) for s in st]
    inv = 1.0 / (ex[0] + ex[1] + ex[2])
    e = e_ref[...]
    merged = None
    for o_ref, x in zip((o0_ref, o1_ref, o2_ref), ex):
        w = x * inv
        w_hi = w.astype(BF16)
        w_lo = (w - w_hi.astype(F32)).astype(BF16)
        wexp = _dot(w_hi, e) + _dot(w_lo, e)
        term = wexp * o_ref[0].astype(F32)
        merged = term if merged is None else merged + term
    act = merged * _silu(gate_ref[0].astype(F32))
    y = _dot(act.astype(BF16), w_out_ref[...])
    out_ref[0] = _deepnorm_ln(h_ref[0], y, g_ref[...], b_ref[...])


def _merge_layer(outs, stats, gate, h, w_out, ln_g, ln_b, *, tile=512):
    B, S, D = h.shape
    e = _head_expand_matrix()
    row = lambda width: pl.BlockSpec((1, tile, width), lambda b, s: (b, s, 0))
    return pl.pallas_call(
        _merge_kernel,
        grid=(B, S // tile),
        in_specs=[row(D_B)] * 3 + [row(LANES)] * 3 + [row(D_B), row(D)] + [
            _const_spec(e.shape), _const_spec(w_out.shape),
            _const_spec(ln_g.shape), _const_spec(ln_b.shape)],
        out_specs=row(D),
        out_shape=jax.ShapeDtypeStruct((B, S, D), F32),
        compiler_params=pltpu.CompilerParams(
            dimension_semantics=("arbitrary", "arbitrary"),
            vmem_limit_bytes=VMEM_LIMIT_BYTES),
        name="merge_layer",
    )(*outs, *stats, gate, h, e, w_out, ln_g, ln_b)


def kernel(x, w_in_a, w_grp_a, scale_a, w_out_a, w_kv, w_in_b, w_out_b, ln_g, ln_b):
    n_q = N_GROUPS * D_B
    B, S, _ = x.shape
    dils = [d for _, d in ATTN_PAIRS]
    q_scale = jnp.concatenate([jnp.full((n_q,), HEAD_DIM ** -0.5 * LOG2_E, F32), jnp.ones((D_B,), F32)])
    kv_plan = ([(grp * D_B, d) for grp, d in enumerate(dils)]
               + [(n_q + grp * D_B, d) for grp, d in enumerate(dils)])
    q_plan = [(grp * D_B, d) for grp, d in enumerate(dils)] + [(n_q, 1)]
    h = x
    ks = vs = None
    for l in range(DEPTH):
        g = ln_g[l][None, :]
        b = ln_b[l][None, :]
        if l < N_A_LAYERS:
            h = _pool_layer(h, w_in_a[l].astype(BF16), w_grp_a[l].astype(BF16),
                            scale_a[l][None, :], w_out_a[l].astype(BF16), g, b)
            if l == N_A_LAYERS - 1:
                kv = _project(h, w_kv.astype(BF16), kv_plan, [BF16] * (2 * N_GROUPS))
                ks, vs = kv[:N_GROUPS], kv[N_GROUPS:]
        else:
            j = l - N_A_LAYERS
            *qs, gate = _project(h, (w_in_b[j] * q_scale).astype(BF16), q_plan,
                                 [BF16] * (N_GROUPS + 1))
            outs, stats = zip(*[_attention_group(qs[grp], ks[grp], vs[grp], grp)
                                for grp in range(N_GROUPS)])
            h = _merge_layer(outs, stats, gate.reshape(B, S, D_B), h, w_out_b[j].astype(BF16), g, b)
    return h
```

```python
import functools

import jax
import jax.numpy as jnp
from jax import lax
from jax.experimental import pallas as pl
from jax.experimental.pallas import tpu as pltpu

D_MODEL = 1024
DEPTH = 4
N_A_LAYERS = DEPTH // 2
D_A = 2 * D_MODEL
POOL_WINDOWS = (2, 4, 8, 16)
N_POOL_GROUPS = len(POOL_WINDOWS)
POOL_GW = D_A // N_POOL_GROUPS
ATTN_PAIRS = ((128, 1), (512, 4), (2048, 16))
N_GROUPS = len(ATTN_PAIRS)
N_HEADS = 16
HEAD_DIM = D_MODEL // N_HEADS
D_B = N_HEADS * HEAD_DIM
BLK = 128
DN_ALPHA = (2 * DEPTH) ** 0.25
LN_EPS = 1e-5
NEG_INF = -1e30
LOG2_E = 1.4426950408889634

LANES = 128
N_PAIRS = N_HEADS // 2
SUBLANES = 8
POOL_HALO = SUBLANES * N_POOL_GROUPS
VMEM_LIMIT_BYTES = 56 * 1024 * 1024

F32 = jnp.float32
BF16 = jnp.bfloat16


def _dot(a, b):
    return jnp.dot(a, b, preferred_element_type=F32)


def _silu(x):
    hx = 0.5 * x
    return hx + hx * jnp.tanh(hx)


def _deepnorm_ln(h, y, g, b):
    z = DN_ALPHA * h + y
    mu = jnp.mean(z, axis=-1, keepdims=True)
    zc = z - mu
    var = jnp.mean(zc * zc, axis=-1, keepdims=True)
    return zc * lax.rsqrt(var + LN_EPS) * g + b


def _const_spec(shape):
    nd = len(shape)
    return pl.BlockSpec(shape, lambda *_: (0,) * nd, pipeline_mode=pl.Buffered(1))


def _pool_layer_kernel(h_ref, w_in_ref, w_grp_ref, scale_ref, w_out_ref, g_ref, b_ref,
                       o_ref, hext_ref, *sum_refs, tile):
    s = pl.program_id(1)
    halo, end = POOL_HALO, POOL_HALO + tile

    @pl.when(s == 0)
    def _():
        hext_ref[0:halo, :] = jnp.zeros((halo, D_MODEL), F32)

    h = h_ref[0]
    hb = h.astype(BF16)
    hext_ref[halo:end, :] = h
    src, w, lo = hext_ref, 1, 0
    window_sums = []
    for dst in sum_refs:
        lo += SUBLANES
        dst[lo:end, :] = src[lo:end, :] + src[lo - w:end - w, :]
        window_sums.append(dst[halo:end, :])
        src, w = dst, 2 * w
    window_sums.append(src[halo:end, :] + src[halo - w:end - w, :])

    pos = s * tile + lax.broadcasted_iota(jnp.int32, (tile, 1), 0)
    y = jnp.zeros((tile, D_MODEL), F32)
    for g, (w, wsum) in enumerate(zip(POOL_WINDOWS, window_sums)):
        c0 = g * POOL_GW
        inv_cnt = 1.0 / jnp.minimum(pos + 1, w).astype(F32)
        lhs = (wsum * inv_cnt - h).astype(BF16)
        pooled = _dot(lhs, w_in_ref[:, c0:c0 + POOL_GW])
        mixed = _dot(pooled.astype(BF16), w_grp_ref[g]) * scale_ref[:, c0:c0 + POOL_GW]
        gate = _dot(hb, w_in_ref[:, D_A + c0:D_A + c0 + POOL_GW])
        act = mixed * _silu(gate)
        y = y + _dot(act.astype(BF16), w_out_ref[c0:c0 + POOL_GW, :])
    hext_ref[0:halo, :] = hext_ref[tile:end, :]
    o_ref[0] = _deepnorm_ln(h, y, g_ref[...], b_ref[...])


def _pool_layer(h, w_in, w_grp, scale, w_out, ln_g, ln_b, *, tile=512):
    B, S, D = h.shape
    kern = functools.partial(_pool_layer_kernel, tile=tile)
    return pl.pallas_call(
        kern,
        grid=(B, S // tile),
        in_specs=[
            pl.BlockSpec((1, tile, D), lambda b, s: (b, s, 0)),
            _const_spec(w_in.shape),
            _const_spec(w_grp.shape),
            _const_spec(scale.shape),
            _const_spec(w_out.shape),
            _const_spec(ln_g.shape),
            _const_spec(ln_b.shape),
        ],
        out_specs=pl.BlockSpec((1, tile, D), lambda b, s: (b, s, 0)),
        out_shape=jax.ShapeDtypeStruct((B, S, D), F32),
        scratch_shapes=[pltpu.VMEM((POOL_HALO + tile, D), F32)] * N_POOL_GROUPS,
        compiler_params=pltpu.CompilerParams(
            dimension_semantics=("arbitrary", "arbitrary"),
            vmem_limit_bytes=VMEM_LIMIT_BYTES),
        name="pool_layer",
    )(h, w_in, w_grp, scale, w_out, ln_g, ln_b)


def _proj_kernel(x_ref, w_ref, *refs, plan, tile):
    out_refs, slab_ref = refs[:len(plan)], refs[len(plan)]
    x = x_ref[0]
    lhs = {1: x.astype(BF16)}
    dils = sorted({d for _, d in plan if d > 1})
    if dils:
        for j in range(D_MODEL // LANES):
            slab_ref[j] = x[:, j * LANES:(j + 1) * LANES]
    for d in dils:
        planes = [
            jnp.concatenate([slab_ref[j, pl.ds(r, tile // d, stride=d), :]
                             for j in range(D_MODEL // LANES)], axis=1)
            for r in range(d)]
        lhs[d] = jnp.concatenate(planes, axis=0).astype(BF16)
    for (c0, d), o_ref in zip(plan, out_refs):
        res = _dot(lhs[d], w_ref[:, c0:c0 + D_B]).astype(o_ref.dtype)
        rows = tile // d
        for r in range(d):
            o_ref[0, r] = res[r * rows:(r + 1) * rows]


def _project(x, w, plan, dtypes, *, tile=512):
    B, S, D = x.shape
    out_shape = [jax.ShapeDtypeStruct((B, d, S // d, D_B), dt) for (_, d), dt in zip(plan, dtypes)]
    out_specs = [pl.BlockSpec((1, d, tile // d, D_B), lambda b, s: (b, 0, s, 0)) for _, d in plan]
    kern = functools.partial(_proj_kernel, plan=tuple(plan), tile=tile)
    return pl.pallas_call(
        kern,
        grid=(B, S // tile),
        in_specs=[pl.BlockSpec((1, tile, D), lambda b, s: (b, s, 0)), _const_spec(w.shape)],
        out_specs=out_specs,
        out_shape=out_shape,
        scratch_shapes=[pltpu.VMEM((D // LANES, tile, LANES), F32)],
        compiler_params=pltpu.CompilerParams(
            dimension_semantics=("arbitrary", "arbitrary"),
            vmem_limit_bytes=VMEM_LIMIT_BYTES),
        name="project",
    )(x, w)


def _attn_bias(group, dilation):
    n = N_GROUPS * N_HEADS
    i = jnp.arange(1, n + 1, dtype=F32)
    slopes = (2.0 ** (-8.0 * i / n)).reshape(N_GROUPS, N_HEADS)[group]
    a = jnp.arange(BLK)[:, None]
    c = jnp.arange(2 * BLK)[None, :]
    rel = BLK + a - c
    win_sub = ATTN_PAIRS[group][0] // dilation
    valid = (rel >= 0) & (rel <= win_sub)
    bias = -slopes[:, None, None] * (dilation * rel).astype(F32)[None] * LOG2_E
    later = jnp.where(valid[None], bias, NEG_INF)
    first = jnp.where((valid & (c >= BLK))[None], bias, NEG_INF)
    return jnp.stack([later, first], axis=0).reshape(2, N_PAIRS, 2 * BLK, 2 * BLK)


def _attn_kernel(q_ref, kp_ref, kc_ref, vp_ref, vc_ref, bias_ref, o_ref, st_ref, *scratch,
                 dilation, res_per_step, blk_per_step):
    n = pl.program_id(1)
    rg = pl.program_id(2)
    lane = lax.broadcasted_iota(jnp.int32, (1, LANES), 1)
    lo = lane < HEAD_DIM
    ones = jnp.ones((2 * BLK, LANES), BF16)

    def block(rr, j, key_window, bsel):
        row0 = j * BLK if isinstance(j, int) else pl.multiple_of(j * BLK, BLK)
        out0 = row0 * dilation + rg * res_per_step + rr
        stats = jnp.zeros((BLK, LANES), F32)
        for p in range(N_PAIRS):
            cols = slice(p * LANES, (p + 1) * LANES)
            qp = q_ref[0, rr, pl.ds(row0, BLK), cols]
            kp = key_window(kp_ref, kc_ref, cols)
            vp = key_window(vp_ref, vc_ref, cols)
            zero = jnp.zeros_like(qp)
            q2 = jnp.concatenate([jnp.where(lo, qp, zero), jnp.where(lo, zero, qp)], axis=0)
            sc = lax.dot_general(q2, kp, (((1,), (1,)), ((), ())), preferred_element_type=F32)
            sc = sc + bias_ref[bsel, p]
            m = jnp.max(sc, axis=-1, keepdims=True)
            pe = jnp.exp2((sc - m).astype(BF16))
            acc = _dot(pe, jnp.concatenate([vp, ones], axis=1))
            pick = lambda x: jnp.where(lo, x[:BLK], x[BLK:])
            denom = pick(acc[:, LANES:])
            m_pair = pick(m)
            o = pick(acc[:, :LANES]) / denom
            if dilation == 1:
                o_ref[0, pl.ds(row0, BLK), cols] = o.astype(BF16)
            else:
                scratch[0][p, pl.ds(out0, BLK, stride=dilation), :] = o
            lse = m_pair + jnp.log2(denom)
            stats = jnp.where((lane & (HEAD_DIM - 1)) == p, lse, stats)
        if dilation == 1:
            st_ref[0, pl.ds(row0, BLK), :] = stats
        else:
            st_ref[0, pl.ds(out0, BLK, stride=dilation), :] = stats

    first_tile = (n == 0).astype(jnp.int32)

    def first_block(rr, carry):
        window = lambda prev, cur, cols: jnp.concatenate(
            [prev[0, rr, :, cols], cur[0, rr, 0:BLK, cols]], axis=0)
        block(rr, 0, window, first_tile)
        return carry

    lax.fori_loop(0, res_per_step, first_block, 0, unroll=min(2, res_per_step))

    rest = blk_per_step - 1
    if rest:
        def later_block(i, carry):
            rr, j = i // rest, 1 + i % rest
            k0 = pl.multiple_of((j - 1) * BLK, BLK)
            window = lambda prev, cur, cols: cur[0, rr, pl.ds(k0, 2 * BLK), cols]
            block(rr, j, window, 0)
            return carry

        lax.fori_loop(0, res_per_step * rest, later_block, 0, unroll=2)

    if dilation > 1:
        @pl.when(rg == dilation // res_per_step - 1)
        def _():
            for p in range(N_PAIRS):
                o_ref[0, :, p * LANES:(p + 1) * LANES] = scratch[0][p].astype(BF16)


ATTN_STEP_SHAPE = {1: (1, 8), 4: (4, 2), 16: (8, 1)}


def _attention_group(q, k, v, group):
    _, d = ATTN_PAIRS[group]
    B, _, L, _ = q.shape
    S = L * d
    rb, qb = ATTN_STEP_SHAPE[d]
    rows = qb * BLK
    bias = _attn_bias(group, d)
    cur_spec = pl.BlockSpec((1, rb, rows, D_B), lambda b, n, rg: (b, rg, n, 0))
    prev_spec = pl.BlockSpec((1, rb, BLK, D_B), lambda b, n, rg: (b, rg, jnp.maximum(n * qb - 1, 0), 0))
    kern = functools.partial(_attn_kernel, dilation=d, res_per_step=rb, blk_per_step=qb)
    return pl.pallas_call(
        kern,
        grid=(B, L // rows, d // rb),
        in_specs=[cur_spec, prev_spec, cur_spec, prev_spec, cur_spec, _const_spec(bias.shape)],
        out_specs=[
            pl.BlockSpec((1, rows * d, D_B), lambda b, n, rg: (b, n, 0)),
            pl.BlockSpec((1, rows * d, LANES), lambda b, n, rg: (b, n, 0)),
        ],
        out_shape=[
            jax.ShapeDtypeStruct((B, S, D_B), BF16),
            jax.ShapeDtypeStruct((B, S, LANES), F32),
        ],
        scratch_shapes=[pltpu.VMEM((N_PAIRS, rows * d, LANES), F32)] if d > 1 else [],
        compiler_params=pltpu.CompilerParams(
            dimension_semantics=("arbitrary", "arbitrary", "arbitrary"),
            vmem_limit_bytes=VMEM_LIMIT_BYTES),
        name=f"attn_g{group}",
    )(q, k, k, v, v, bias)


def _head_expand_matrix():
    row = jnp.arange(LANES)[:, None]
    col = jnp.arange(D_B)[None, :]
    head = col // HEAD_DIM
    src = (head // 2) + (head % 2) * HEAD_DIM
    return (row == src).astype(BF16)


def _merge_kernel(o0_ref, o1_ref, o2_ref, s0_ref, s1_ref, s2_ref, gate_ref, h_ref,
                  e_ref, w_out_ref, g_ref, b_ref, out_ref):
    st = [s0_ref[0], s1_ref[0], s2_ref[0]]
    mx = jnp.maximum(jnp.maximum(st[0], st[1]), st[2])
    ex = [jnp.exp2(s - mx) for s in st]
    inv = 1.0 / (ex[0] + ex[1] + ex[2])
    e = e_ref[...]
    w0 = _dot((ex[0] * inv).astype(BF16), e)
    w1 = _dot((ex[1] * inv).astype(BF16), e)
    w2 = 1.0 - (w0 + w1)
    merged = (w0 * o0_ref[0].astype(F32) + w1 * o1_ref[0].astype(F32)
              + w2 * o2_ref[0].astype(F32))
    act = merged * _silu(gate_ref[0].astype(F32))
    y = _dot(act.astype(BF16), w_out_ref[...])
    out_ref[0] = _deepnorm_ln(h_ref[0], y, g_ref[...], b_ref[...])


def _merge_layer(outs, stats, gate, h, w_out, ln_g, ln_b, *, tile=512):
    B, S, D = h.shape
    e = _head_expand_matrix()
    row = lambda width: pl.BlockSpec((1, tile, width), lambda b, s: (b, s, 0))
    return pl.pallas_call(
        _merge_kernel,
        grid=(B, S // tile),
        in_specs=[row(D_B)] * 3 + [row(LANES)] * 3 + [row(D_B), row(D)] + [
            _const_spec(e.shape), _const_spec(w_out.shape),
            _const_spec(ln_g.shape), _const_spec(ln_b.shape)],
        out_specs=row(D),
        out_shape=jax.ShapeDtypeStruct((B, S, D), F32),
        compiler_params=pltpu.CompilerParams(
            dimension_semantics=("arbitrary", "arbitrary"),
            vmem_limit_bytes=VMEM_LIMIT_BYTES),
        name="merge_layer",
    )(*outs, *stats, gate, h, e, w_out, ln_g, ln_b)


def kernel(x, w_in_a, w_grp_a, scale_a, w_out_a, w_kv, w_in_b, w_out_b, ln_g, ln_b):
    n_q = N_GROUPS * D_B
    B, S, _ = x.shape
    dils = [d for _, d in ATTN_PAIRS]
    q_scale = jnp.concatenate([jnp.full((n_q,), HEAD_DIM ** -0.5 * LOG2_E, F32), jnp.ones((D_B,), F32)])
    kv_plan = ([(grp * D_B, d) for grp, d in enumerate(dils)]
               + [(n_q + grp * D_B, d) for grp, d in enumerate(dils)])
    q_plan = [(grp * D_B, d) for grp, d in enumerate(dils)] + [(n_q, 1)]
    h = x
    ks = vs = None
    for l in range(DEPTH):
        g = ln_g[l][None, :]
        b = ln_b[l][None, :]
        if l < N_A_LAYERS:
            h = _pool_layer(h, w_in_a[l].astype(BF16), w_grp_a[l].astype(BF16),
                            scale_a[l][None, :], w_out_a[l].astype(BF16), g, b)
            if l == N_A_LAYERS - 1:
                kv = _project(h, w_kv.astype(BF16), kv_plan, [BF16] * (2 * N_GROUPS))
                ks, vs = kv[:N_GROUPS], kv[N_GROUPS:]
        else:
            j = l - N_A_LAYERS
            *qs, gate = _project(h, (w_in_b[j] * q_scale).astype(BF16), q_plan,
                                 [BF16] * (N_GROUPS + 1))
            outs, stats = zip(*[_attention_group(qs[grp], ks[grp], vs[grp], grp)
                                for grp in range(N_GROUPS)])
            h = _merge_layer(outs, stats, gate.reshape(B, S, D_B), h, w_out_b[j].astype(BF16), g, b)
    return h
```

```python
import functools

import jax
import jax.numpy as jnp
from jax import lax
from jax.experimental import pallas as pl
from jax.experimental.pallas import tpu as pltpu

D_MODEL = 1024
DEPTH = 4
N_A_LAYERS = DEPTH // 2
D_A = 2 * D_MODEL
POOL_WINDOWS = (2, 4, 8, 16)
N_POOL_GROUPS = len(POOL_WINDOWS)
POOL_GW = D_A // N_POOL_GROUPS
ATTN_PAIRS = ((128, 1), (512, 4), (2048, 16))
N_GROUPS = len(ATTN_PAIRS)
N_HEADS = 16
HEAD_DIM = D_MODEL // N_HEADS
D_B = N_HEADS * HEAD_DIM
BLK = 128
DN_ALPHA = (2 * DEPTH) ** 0.25
LN_EPS = 1e-5
NEG_INF = -1e30
LOG2_E = 1.4426950408889634

LANES = 128
N_PAIRS = N_HEADS // 2
SUBLANES = 8
POOL_HALO = SUBLANES * N_POOL_GROUPS
VMEM_LIMIT_BYTES = 56 * 1024 * 1024
ATTN_UNROLL = 4

F32 = jnp.float32
BF16 = jnp.bfloat16


def _dot(a, b):
    return jnp.dot(a, b, preferred_element_type=F32)


def _silu(x):
    hx = 0.5 * x
    return hx + hx * jnp.tanh(hx)


def _deepnorm_ln(h, y, g, b):
    z = DN_ALPHA * h + y
    mu = jnp.mean(z, axis=-1, keepdims=True)
    zc = z - mu
    var = jnp.mean(zc * zc, axis=-1, keepdims=True)
    return zc * lax.rsqrt(var + LN_EPS) * g + b


def _const_spec(shape):
    nd = len(shape)
    return pl.BlockSpec(shape, lambda *_: (0,) * nd, pipeline_mode=pl.Buffered(1))


def _fold_kernel(w_u_ref, w_grp_ref, scale_ref, o_ref):
    o_ref[...] = (_dot(w_u_ref[...], w_grp_ref[0]) * scale_ref[...]).astype(BF16)


def _fold_pool_weights(w_in, w_grp, scale):
    D = w_in.shape[0]
    return pl.pallas_call(
        _fold_kernel,
        grid=(N_POOL_GROUPS,),
        in_specs=[
            pl.BlockSpec((D, POOL_GW), lambda g: (0, g)),
            pl.BlockSpec((1, POOL_GW, POOL_GW), lambda g: (g, 0, 0)),
            pl.BlockSpec((1, POOL_GW), lambda g: (0, g)),
        ],
        out_specs=pl.BlockSpec((D, POOL_GW), lambda g: (0, g)),
        out_shape=jax.ShapeDtypeStruct((D, D_A), BF16),
        compiler_params=pltpu.CompilerParams(dimension_semantics=("arbitrary",)),
        name="fold_pool_weights",
    )(w_in, w_grp, scale)


def _pool_layer_kernel(h_ref, w_mix_ref, w_gate_ref, w_out_ref, g_ref, b_ref,
                       o_ref, hext_ref, *sum_refs, tile):
    s = pl.program_id(1)
    halo, end = POOL_HALO, POOL_HALO + tile

    @pl.when(s == 0)
    def _():
        hext_ref[0:halo, :] = jnp.zeros((halo, D_MODEL), F32)

    h = h_ref[0]
    hb = h.astype(BF16)
    gates = [_dot(hb, w_gate_ref[:, g * POOL_GW:(g + 1) * POOL_GW]) for g in range(N_POOL_GROUPS)]
    hext_ref[halo:end, :] = h
    src, w, lo = hext_ref, 1, 0
    window_sums = []
    for dst in sum_refs:
        lo += SUBLANES
        dst[lo:end, :] = src[lo:end, :] + src[lo - w:end - w, :]
        window_sums.append(dst[halo:end, :])
        src, w = dst, 2 * w
    window_sums.append(src[halo:end, :] + src[halo - w:end - w, :])

    pos = s * tile + lax.broadcasted_iota(jnp.int32, (tile, 1), 0)
    y = jnp.zeros((tile, D_MODEL), F32)
    for g, (w, wsum) in enumerate(zip(POOL_WINDOWS, window_sums)):
        c0 = g * POOL_GW
        inv_cnt = 1.0 / jnp.minimum(pos + 1, w).astype(F32)
        lhs = (wsum * inv_cnt - h).astype(BF16)
        mixed = _dot(lhs, w_mix_ref[:, c0:c0 + POOL_GW])
        act = mixed * _silu(gates[g])
        y = y + _dot(act.astype(BF16), w_out_ref[c0:c0 + POOL_GW, :])
    hext_ref[0:halo, :] = hext_ref[tile:end, :]
    o_ref[0] = _deepnorm_ln(h, y, g_ref[...], b_ref[...])


def _pool_layer(h, w_in, w_grp, scale, w_out, ln_g, ln_b, *, tile=512):
    B, S, D = h.shape
    w_mix = _fold_pool_weights(w_in, w_grp, scale)
    kern = functools.partial(_pool_layer_kernel, tile=tile)
    return pl.pallas_call(
        kern,
        grid=(B, S // tile),
        in_specs=[
            pl.BlockSpec((1, tile, D), lambda b, s: (b, s, 0)),
            _const_spec(w_mix.shape),
            pl.BlockSpec((D, D_A), lambda b, s: (0, 1), pipeline_mode=pl.Buffered(1)),
            _const_spec(w_out.shape),
            _const_spec(ln_g.shape),
            _const_spec(ln_b.shape),
        ],
        out_specs=pl.BlockSpec((1, tile, D), lambda b, s: (b, s, 0)),
        out_shape=jax.ShapeDtypeStruct((B, S, D), F32),
        scratch_shapes=[pltpu.VMEM((POOL_HALO + tile, D), F32)] * N_POOL_GROUPS,
        compiler_params=pltpu.CompilerParams(
            dimension_semantics=("arbitrary", "arbitrary"),
            vmem_limit_bytes=VMEM_LIMIT_BYTES),
        name="pool_layer",
    )(h, w_mix, w_in, w_out, ln_g, ln_b)


def _proj_kernel(x_ref, w_ref, *refs, plan, tile):
    out_refs, slab_ref = refs[:len(plan)], refs[len(plan)]
    x = x_ref[0]
    lhs = {1: x.astype(BF16)}
    dils = sorted({d for _, d in plan if d > 1})
    if dils:
        for j in range(D_MODEL // LANES):
            slab_ref[j] = x[:, j * LANES:(j + 1) * LANES]
    for d in dils:
        planes = [
            jnp.concatenate([slab_ref[j, pl.ds(r, tile // d, stride=d), :]
                             for j in range(D_MODEL // LANES)], axis=1)
            for r in range(d)]
        lhs[d] = jnp.concatenate(planes, axis=0).astype(BF16)
    for (c0, d), o_ref in zip(plan, out_refs):
        res = _dot(lhs[d], w_ref[:, c0:c0 + D_B]).astype(o_ref.dtype)
        rows = tile // d
        for r in range(d):
            o_ref[0, r] = res[r * rows:(r + 1) * rows]


def _project(x, w, plan, dtypes, *, tile=512):
    B, S, D = x.shape
    out_shape = [jax.ShapeDtypeStruct((B, d, S // d, D_B), dt) for (_, d), dt in zip(plan, dtypes)]
    out_specs = [pl.BlockSpec((1, d, tile // d, D_B), lambda b, s: (b, 0, s, 0)) for _, d in plan]
    kern = functools.partial(_proj_kernel, plan=tuple(plan), tile=tile)
    return pl.pallas_call(
        kern,
        grid=(B, S // tile),
        in_specs=[pl.BlockSpec((1, tile, D), lambda b, s: (b, s, 0)), _const_spec(w.shape)],
        out_specs=out_specs,
        out_shape=out_shape,
        scratch_shapes=[pltpu.VMEM((D // LANES, tile, LANES), F32)],
        compiler_params=pltpu.CompilerParams(
            dimension_semantics=("arbitrary", "arbitrary"),
            vmem_limit_bytes=VMEM_LIMIT_BYTES),
        name="project",
    )(x, w)


def _attn_bias(group, dilation):
    n = N_GROUPS * N_HEADS
    i = jnp.arange(1, n + 1, dtype=F32)
    slopes = (2.0 ** (-8.0 * i / n)).reshape(N_GROUPS, N_HEADS)[group]
    a = jnp.arange(BLK)[:, None]
    c = jnp.arange(2 * BLK)[None, :]
    rel = BLK + a - c
    win_sub = ATTN_PAIRS[group][0] // dilation
    valid = (rel >= 0) & (rel <= win_sub)
    bias = -slopes[:, None, None] * (dilation * rel).astype(F32)[None] * LOG2_E
    later = jnp.where(valid[None], bias, NEG_INF)
    first = jnp.where((valid & (c >= BLK))[None], bias, NEG_INF)
    return jnp.stack([later, first], axis=0).reshape(2, N_PAIRS, 2 * BLK, 2 * BLK)


def _attn_kernel(q_ref, kp_ref, kc_ref, vp_ref, vc_ref, bias_ref, o_ref, st_ref, *scratch,
                 dilation, res_per_step, blk_per_step):
    n = pl.program_id(1)
    rg = pl.program_id(2)
    lane = lax.broadcasted_iota(jnp.int32, (1, LANES), 1)
    lo = lane < HEAD_DIM
    ones = jnp.ones((2 * BLK, LANES), BF16)

    def block(rr, j, key_window, bsel):
        row0 = j * BLK if isinstance(j, int) else pl.multiple_of(j * BLK, BLK)
        out0 = row0 * dilation + rg * res_per_step + rr
        stats = jnp.zeros((BLK, LANES), F32)
        for p in range(N_PAIRS):
            cols = slice(p * LANES, (p + 1) * LANES)
            qp = q_ref[0, rr, pl.ds(row0, BLK), cols]
            kp = key_window(kp_ref, kc_ref, cols)
            vp = key_window(vp_ref, vc_ref, cols)
            zero = jnp.zeros_like(qp)
            q2 = jnp.concatenate([jnp.where(lo, qp, zero), jnp.where(lo, zero, qp)], axis=0)
            sc = lax.dot_general(q2, kp, (((1,), (1,)), ((), ())), preferred_element_type=F32)
            sc = sc + bias_ref[bsel, p]
            m = jnp.max(sc, axis=-1, keepdims=True)
            pe = jnp.exp2((sc - m).astype(BF16))
            acc = _dot(pe, jnp.concatenate([vp, ones], axis=1))
            pick = lambda x: jnp.where(lo, x[:BLK], x[BLK:])
            denom = pick(acc[:, LANES:])
            m_pair = pick(m)
            o = pick(acc[:, :LANES]) / denom
            if dilation == 1:
                o_ref[0, pl.ds(row0, BLK), cols] = o.astype(BF16)
            else:
                scratch[0][p, pl.ds(out0, BLK, stride=dilation), :] = o
            lse = m_pair + jnp.log2(denom)
            stats = jnp.where((lane & (HEAD_DIM - 1)) == p, lse, stats)
        if dilation == 1:
            st_ref[0, pl.ds(row0, BLK), :] = stats
        else:
            st_ref[0, pl.ds(out0, BLK, stride=dilation), :] = stats

    first_tile = (n == 0).astype(jnp.int32)

    def first_block(rr, carry):
        window = lambda prev, cur, cols: jnp.concatenate(
            [prev[0, rr, :, cols], cur[0, rr, 0:BLK, cols]], axis=0)
        block(rr, 0, window, first_tile)
        return carry

    lax.fori_loop(0, res_per_step, first_block, 0, unroll=min(ATTN_UNROLL, res_per_step))

    rest = blk_per_step - 1
    if rest:
        def later_block(i, carry):
            rr, j = i // rest, 1 + i % rest
            k0 = pl.multiple_of((j - 1) * BLK, BLK)
            window = lambda prev, cur, cols: cur[0, rr, pl.ds(k0, 2 * BLK), cols]
            block(rr, j, window, 0)
            return carry

        lax.fori_loop(0, res_per_step * rest, later_block, 0, unroll=ATTN_UNROLL)

    if dilation > 1:
        @pl.when(rg == dilation // res_per_step - 1)
        def _():
            for p in range(N_PAIRS):
                o_ref[0, :, p * LANES:(p + 1) * LANES] = scratch[0][p].astype(BF16)


ATTN_STEP_SHAPE = {1: (1, 8), 4: (4, 2), 16: (8, 1)}


def _attention_group(q, k, v, group):
    _, d = ATTN_PAIRS[group]
    B, _, L, _ = q.shape
    S = L * d
    rb, qb = ATTN_STEP_SHAPE[d]
    rows = qb * BLK
    bias = _attn_bias(group, d)
    cur_spec = pl.BlockSpec((1, rb, rows, D_B), lambda b, n, rg: (b, rg, n, 0))
    prev_spec = pl.BlockSpec((1, rb, BLK, D_B), lambda b, n, rg: (b, rg, jnp.maximum(n * qb - 1, 0), 0))
    kern = functools.partial(_attn_kernel, dilation=d, res_per_step=rb, blk_per_step=qb)
    return pl.pallas_call(
        kern,
        grid=(B, L // rows, d // rb),
        in_specs=[cur_spec, prev_spec, cur_spec, prev_spec, cur_spec, _const_spec(bias.shape)],
        out_specs=[
            pl.BlockSpec((1, rows * d, D_B), lambda b, n, rg: (b, n, 0)),
            pl.BlockSpec((1, rows * d, LANES), lambda b, n, rg: (b, n, 0)),
        ],
        out_shape=[
            jax.ShapeDtypeStruct((B, S, D_B), BF16),
            jax.ShapeDtypeStruct((B, S, LANES), F32),
        ],
        scratch_shapes=[pltpu.VMEM((N_PAIRS, rows * d, LANES), F32)] if d > 1 else [],
        compiler_params=pltpu.CompilerParams(
            dimension_semantics=("arbitrary", "arbitrary", "arbitrary"),
            vmem_limit_bytes=VMEM_LIMIT_BYTES),
        name=f"attn_g{group}",
    )(q, k, k, v, v, bias)


def _head_expand_matrix():
    row = jnp.arange(LANES)[:, None]
    col = jnp.arange(D_B)[None, :]
    head = col // HEAD_DIM
    src = (head // 2) + (head % 2) * HEAD_DIM
    return (row == src).astype(BF16)


def _merge_kernel(o0_ref, o1_ref, o2_ref, s0_ref, s1_ref, s2_ref, gate_ref, h_ref,
                  e_ref, w_out_ref, g_ref, b_ref, out_ref):
    st = [s0_ref[0], s1_ref[0], s2_ref[0]]
    mx = jnp.maximum(jnp.maximum(st[0], st[1]), st[2])
    ex = [jnp.exp2(s - mx) for s in st]
    inv = 1.0 / (ex[0] + ex[1] + ex[2])
    e = e_ref[...]
    w0 = _dot((ex[0] * inv).astype(BF16), e)
    w1 = _dot((ex[1] * inv).astype(BF16), e)
    w2 = 1.0 - (w0 + w1)
    merged = (w0 * o0_ref[0].astype(F32) + w1 * o1_ref[0].astype(F32)
              + w2 * o2_ref[0].astype(F32))
    act = merged * _silu(gate_ref[0].astype(F32))
    y = _dot(act.astype(BF16), w_out_ref[...])
    out_ref[0] = _deepnorm_ln(h_ref[0], y, g_ref[...], b_ref[...])


def _merge_layer(outs, stats, gate, h, w_out, ln_g, ln_b, *, tile=512):
    B, S, D = h.shape
    e = _head_expand_matrix()
    row = lambda width: pl.BlockSpec((1, tile, width), lambda b, s: (b, s, 0))
    return pl.pallas_call(
        _merge_kernel,
        grid=(B, S // tile),
        in_specs=[row(D_B)] * 3 + [row(LANES)] * 3 + [row(D_B), row(D)] + [
            _const_spec(e.shape), _const_spec(w_out.shape),
            _const_spec(ln_g.shape), _const_spec(ln_b.shape)],
        out_specs=row(D),
        out_shape=jax.ShapeDtypeStruct((B, S, D), F32),
        compiler_params=pltpu.CompilerParams(
            dimension_semantics=("arbitrary", "arbitrary"),
            vmem_limit_bytes=VMEM_LIMIT_BYTES),
        name="merge_layer",
    )(*outs, *stats, gate, h, e, w_out, ln_g, ln_b)


def kernel(x, w_in_a, w_grp_a, scale_a, w_out_a, w_kv, w_in_b, w_out_b, ln_g, ln_b):
    n_q = N_GROUPS * D_B
    B, S, _ = x.shape
    dils = [d for _, d in ATTN_PAIRS]
    q_scale = jnp.concatenate([jnp.full((n_q,), HEAD_DIM ** -0.5 * LOG2_E, F32), jnp.ones((D_B,), F32)])
    kv_plan = ([(grp * D_B, d) for grp, d in enumerate(dils)]
               + [(n_q + grp * D_B, d) for grp, d in enumerate(dils)])
    q_plan = [(grp * D_B, d) for grp, d in enumerate(dils)] + [(n_q, 1)]
    h = x
    ks = vs = None
    for l in range(DEPTH):
        g = ln_g[l][None, :]
        b = ln_b[l][None, :]
        if l < N_A_LAYERS:
            h = _pool_layer(h, w_in_a[l].astype(BF16), w_grp_a[l].astype(BF16),
                            scale_a[l][None, :], w_out_a[l].astype(BF16), g, b)
            if l == N_A_LAYERS - 1:
                kv = _project(h, w_kv.astype(BF16), kv_plan, [BF16] * (2 * N_GROUPS))
                ks, vs = kv[:N_GROUPS], kv[N_GROUPS:]
        else:
            j = l - N_A_LAYERS
            *qs, gate = _project(h, (w_in_b[j] * q_scale).astype(BF16), q_plan,
                                 [BF16] * (N_GROUPS + 1))
            outs, stats = zip(*[_attention_group(qs[grp], ks[grp], vs[grp], grp)
                                for grp in range(N_GROUPS)])
            h = _merge_layer(outs, stats, gate.reshape(B, S, D_B), h, w_out_b[j].astype(BF16), g, b)
    return h
```

```python
import functools

import jax
import jax.numpy as jnp
from jax import lax
from jax.experimental import pallas as pl
from jax.experimental.pallas import tpu as pltpu

D_MODEL = 1024
DEPTH = 4
N_A_LAYERS = DEPTH // 2
D_A = 2 * D_MODEL
POOL_WINDOWS = (2, 4, 8, 16)
N_POOL_GROUPS = len(POOL_WINDOWS)
POOL_GW = D_A // N_POOL_GROUPS
ATTN_PAIRS = ((128, 1), (512, 4), (2048, 16))
N_GROUPS = len(ATTN_PAIRS)
N_HEADS = 16
HEAD_DIM = D_MODEL // N_HEADS
D_B = N_HEADS * HEAD_DIM
BLK = 128
DN_ALPHA = (2 * DEPTH) ** 0.25
LN_EPS = 1e-5
NEG_INF = -1e30
LOG2_E = 1.4426950408889634

LANES = 128
N_PAIRS = N_HEADS // 2
SUBLANES = 8
POOL_HALO = SUBLANES * N_POOL_GROUPS
VMEM_LIMIT_BYTES = 56 * 1024 * 1024
ATTN_UNROLL = 8

F32 = jnp.float32
BF16 = jnp.bfloat16


def _dot(a, b):
    return jnp.dot(a, b, preferred_element_type=F32)


def _silu(x):
    hx = 0.5 * x
    return hx + hx * jnp.tanh(hx)


def _deepnorm_ln(h, y, g, b):
    z = DN_ALPHA * h + y
    mu = jnp.mean(z, axis=-1, keepdims=True)
    zc = z - mu
    var = jnp.mean(zc * zc, axis=-1, keepdims=True)
    return zc * lax.rsqrt(var + LN_EPS) * g + b


def _const_spec(shape):
    nd = len(shape)
    return pl.BlockSpec(shape, lambda *_: (0,) * nd, pipeline_mode=pl.Buffered(1))


def _fold_kernel(w_u_ref, w_grp_ref, scale_ref, o_ref):
    o_ref[...] = (_dot(w_u_ref[...], w_grp_ref[0]) * scale_ref[...]).astype(BF16)


def _fold_pool_weights(w_in, w_grp, scale):
    D = w_in.shape[0]
    return pl.pallas_call(
        _fold_kernel,
        grid=(N_POOL_GROUPS,),
        in_specs=[
            pl.BlockSpec((D, POOL_GW), lambda g: (0, g)),
            pl.BlockSpec((1, POOL_GW, POOL_GW), lambda g: (g, 0, 0)),
            pl.BlockSpec((1, POOL_GW), lambda g: (0, g)),
        ],
        out_specs=pl.BlockSpec((D, POOL_GW), lambda g: (0, g)),
        out_shape=jax.ShapeDtypeStruct((D, D_A), BF16),
        compiler_params=pltpu.CompilerParams(dimension_semantics=("arbitrary",)),
        name="fold_pool_weights",
    )(w_in, w_grp, scale)


def _pool_layer_kernel(h_ref, w_mix_ref, w_gate_ref, w_out_ref, g_ref, b_ref,
                       o_ref, hext_ref, *sum_refs, tile):
    s = pl.program_id(1)
    halo, end = POOL_HALO, POOL_HALO + tile

    @pl.when(s == 0)
    def _():
        hext_ref[0:halo, :] = jnp.zeros((halo, D_MODEL), F32)

    h = h_ref[0]
    hb = h.astype(BF16)
    gates = [_dot(hb, w_gate_ref[:, g * POOL_GW:(g + 1) * POOL_GW]) for g in range(N_POOL_GROUPS)]
    hext_ref[halo:end, :] = h
    src, w, lo = hext_ref, 1, 0
    window_sums = []
    for dst in sum_refs:
        lo += SUBLANES
        dst[lo:end, :] = src[lo:end, :] + src[lo - w:end - w, :]
        window_sums.append(dst[halo:end, :])
        src, w = dst, 2 * w
    window_sums.append(src[halo:end, :] + src[halo - w:end - w, :])

    pos = s * tile + lax.broadcasted_iota(jnp.int32, (tile, 1), 0)
    y = jnp.zeros((tile, D_MODEL), F32)
    for g, (w, wsum) in enumerate(zip(POOL_WINDOWS, window_sums)):
        c0 = g * POOL_GW
        inv_cnt = 1.0 / jnp.minimum(pos + 1, w).astype(F32)
        lhs = (wsum * inv_cnt - h).astype(BF16)
        mixed = _dot(lhs, w_mix_ref[:, c0:c0 + POOL_GW])
        act = mixed * _silu(gates[g])
        y = y + _dot(act.astype(BF16), w_out_ref[c0:c0 + POOL_GW, :])
    hext_ref[0:halo, :] = hext_ref[tile:end, :]
    o_ref[0] = _deepnorm_ln(h, y, g_ref[...], b_ref[...])


def _pool_layer(h, w_in, w_grp, scale, w_out, ln_g, ln_b, *, tile=512):
    B, S, D = h.shape
    w_mix = _fold_pool_weights(w_in, w_grp, scale)
    kern = functools.partial(_pool_layer_kernel, tile=tile)
    return pl.pallas_call(
        kern,
        grid=(B, S // tile),
        in_specs=[
            pl.BlockSpec((1, tile, D), lambda b, s: (b, s, 0)),
            _const_spec(w_mix.shape),
            pl.BlockSpec((D, D_A), lambda b, s: (0, 1), pipeline_mode=pl.Buffered(1)),
            _const_spec(w_out.shape),
            _const_spec(ln_g.shape),
            _const_spec(ln_b.shape),
        ],
        out_specs=pl.BlockSpec((1, tile, D), lambda b, s: (b, s, 0)),
        out_shape=jax.ShapeDtypeStruct((B, S, D), F32),
        scratch_shapes=[pltpu.VMEM((POOL_HALO + tile, D), F32)] * N_POOL_GROUPS,
        compiler_params=pltpu.CompilerParams(
            dimension_semantics=("arbitrary", "arbitrary"),
            vmem_limit_bytes=VMEM_LIMIT_BYTES),
        name="pool_layer",
    )(h, w_mix, w_in, w_out, ln_g, ln_b)


def _proj_kernel(x_ref, w_ref, *refs, plan, tile):
    out_refs, slab_ref = refs[:len(plan)], refs[len(plan)]
    x = x_ref[0]
    lhs = {1: x.astype(BF16)}
    dils = sorted({d for _, d in plan if d > 1})
    if dils:
        for j in range(D_MODEL // LANES):
            slab_ref[j] = x[:, j * LANES:(j + 1) * LANES]
    for d in dils:
        planes = [
            jnp.concatenate([slab_ref[j, pl.ds(r, tile // d, stride=d), :]
                             for j in range(D_MODEL // LANES)], axis=1)
            for r in range(d)]
        lhs[d] = jnp.concatenate(planes, axis=0).astype(BF16)
    for (c0, d), o_ref in zip(plan, out_refs):
        res = _dot(lhs[d], w_ref[:, c0:c0 + D_B]).astype(o_ref.dtype)
        rows = tile // d
        for r in range(d):
            o_ref[0, r] = res[r * rows:(r + 1) * rows]


def _project(x, w, plan, dtypes, *, tile=512):
    B, S, D = x.shape
    out_shape = [jax.ShapeDtypeStruct((B, d, S // d, D_B), dt) for (_, d), dt in zip(plan, dtypes)]
    out_specs = [pl.BlockSpec((1, d, tile // d, D_B), lambda b, s: (b, 0, s, 0)) for _, d in plan]
    kern = functools.partial(_proj_kernel, plan=tuple(plan), tile=tile)
    return pl.pallas_call(
        kern,
        grid=(B, S // tile),
        in_specs=[pl.BlockSpec((1, tile, D), lambda b, s: (b, s, 0)), _const_spec(w.shape)],
        out_specs=out_specs,
        out_shape=out_shape,
        scratch_shapes=[pltpu.VMEM((D // LANES, tile, LANES), F32)],
        compiler_params=pltpu.CompilerParams(
            dimension_semantics=("arbitrary", "arbitrary"),
            vmem_limit_bytes=VMEM_LIMIT_BYTES),
        name="project",
    )(x, w)


def _attn_bias(group, dilation):
    n = N_GROUPS * N_HEADS
    i = jnp.arange(1, n + 1, dtype=F32)
    slopes = (2.0 ** (-8.0 * i / n)).reshape(N_GROUPS, N_HEADS)[group]
    a = jnp.arange(BLK)[:, None]
    c = jnp.arange(2 * BLK)[None, :]
    rel = BLK + a - c
    win_sub = ATTN_PAIRS[group][0] // dilation
    valid = (rel >= 0) & (rel <= win_sub)
    bias = -slopes[:, None, None] * (dilation * rel).astype(F32)[None] * LOG2_E
    later = jnp.where(valid[None], bias, NEG_INF)
    first = jnp.where((valid & (c >= BLK))[None], bias, NEG_INF)
    return jnp.stack([later, first], axis=0).reshape(2, N_PAIRS, 2 * BLK, 2 * BLK)


def _attn_kernel(q_ref, kp_ref, kc_ref, vp_ref, vc_ref, bias_ref, o_ref, st_ref, *scratch,
                 dilation, res_per_step, blk_per_step):
    n = pl.program_id(1)
    rg = pl.program_id(2)
    lane = lax.broadcasted_iota(jnp.int32, (1, LANES), 1)
    lo = lane < HEAD_DIM
    ones = jnp.ones((2 * BLK, LANES), BF16)

    def block(rr, j, key_window, bsel):
        row0 = j * BLK if isinstance(j, int) else pl.multiple_of(j * BLK, BLK)
        out0 = row0 * dilation + rg * res_per_step + rr
        stats = jnp.zeros((BLK, LANES), F32)
        for p in range(N_PAIRS):
            cols = slice(p * LANES, (p + 1) * LANES)
            qp = q_ref[0, rr, pl.ds(row0, BLK), cols]
            kp = key_window(kp_ref, kc_ref, cols)
            vp = key_window(vp_ref, vc_ref, cols)
            zero = jnp.zeros_like(qp)
            q2 = jnp.concatenate([jnp.where(lo, qp, zero), jnp.where(lo, zero, qp)], axis=0)
            sc = lax.dot_general(q2, kp, (((1,), (1,)), ((), ())), preferred_element_type=F32)
            sc = sc + bias_ref[bsel, p]
            m = jnp.max(sc, axis=-1, keepdims=True)
            pe = jnp.exp2((sc - m).astype(BF16))
            acc = _dot(pe, jnp.concatenate([vp, ones], axis=1))
            pick = lambda x: jnp.where(lo, x[:BLK], x[BLK:])
            denom = pick(acc[:, LANES:])
            m_pair = pick(m)
            o = pick(acc[:, :LANES]) / denom
            if dilation == 1:
                o_ref[0, pl.ds(row0, BLK), cols] = o.astype(BF16)
            else:
                scratch[0][p, pl.ds(out0, BLK, stride=dilation), :] = o
            lse = m_pair + jnp.log2(denom)
            stats = jnp.where((lane & (HEAD_DIM - 1)) == p, lse, stats)
        if dilation == 1:
            st_ref[0, pl.ds(row0, BLK), :] = stats
        else:
            st_ref[0, pl.ds(out0, BLK, stride=dilation), :] = stats

    first_tile = (n == 0).astype(jnp.int32)

    def first_block(rr, carry):
        window = lambda prev, cur, cols: jnp.concatenate(
            [prev[0, rr, :, cols], cur[0, rr, 0:BLK, cols]], axis=0)
        block(rr, 0, window, first_tile)
        return carry

    lax.fori_loop(0, res_per_step, first_block, 0, unroll=min(ATTN_UNROLL, res_per_step))

    rest = blk_per_step - 1
    if rest:
        def later_block(i, carry):
            rr, j = i // rest, 1 + i % rest
            k0 = pl.multiple_of((j - 1) * BLK, BLK)
            window = lambda prev, cur, cols: cur[0, rr, pl.ds(k0, 2 * BLK), cols]
            block(rr, j, window, 0)
            return carry

        lax.fori_loop(0, res_per_step * rest, later_block, 0, unroll=ATTN_UNROLL)

    if dilation > 1:
        @pl.when(rg == dilation // res_per_step - 1)
        def _():
            for p in range(N_PAIRS):
                o_ref[0, :, p * LANES:(p + 1) * LANES] = scratch[0][p].astype(BF16)


ATTN_STEP_SHAPE = {1: (1, 8), 4: (4, 2), 16: (8, 1)}


def _attention_group(q, k, v, group):
    _, d = ATTN_PAIRS[group]
    B, _, L, _ = q.shape
    S = L * d
    rb, qb = ATTN_STEP_SHAPE[d]
    rows = qb * BLK
    bias = _attn_bias(group, d)
    cur_spec = pl.BlockSpec((1, rb, rows, D_B), lambda b, n, rg: (b, rg, n, 0))
    prev_spec = pl.BlockSpec((1, rb, BLK, D_B), lambda b, n, rg: (b, rg, jnp.maximum(n * qb - 1, 0), 0))
    kern = functools.partial(_attn_kernel, dilation=d, res_per_step=rb, blk_per_step=qb)
    return pl.pallas_call(
        kern,
        grid=(B, L // rows, d // rb),
        in_specs=[cur_spec, prev_spec, cur_spec, prev_spec, cur_spec, _const_spec(bias.shape)],
        out_specs=[
            pl.BlockSpec((1, rows * d, D_B), lambda b, n, rg: (b, n, 0)),
            pl.BlockSpec((1, rows * d, LANES), lambda b, n, rg: (b, n, 0)),
        ],
        out_shape=[
            jax.ShapeDtypeStruct((B, S, D_B), BF16),
            jax.ShapeDtypeStruct((B, S, LANES), F32),
        ],
        scratch_shapes=[pltpu.VMEM((N_PAIRS, rows * d, LANES), F32)] if d > 1 else [],
        compiler_params=pltpu.CompilerParams(
            dimension_semantics=("arbitrary", "arbitrary", "arbitrary"),
            vmem_limit_bytes=VMEM_LIMIT_BYTES),
        name=f"attn_g{group}",
    )(q, k, k, v, v, bias)


def _head_expand_matrix():
    row = jnp.arange(LANES)[:, None]
    col = jnp.arange(D_B)[None, :]
    head = col // HEAD_DIM
    src = (head // 2) + (head % 2) * HEAD_DIM
    return (row == src).astype(BF16)


def _merge_kernel(o0_ref, o1_ref, o2_ref, s0_ref, s1_ref, s2_ref, gate_ref, h_ref,
                  e_ref, w_out_ref, g_ref, b_ref, out_ref):
    st = [s0_ref[0], s1_ref[0], s2_ref[0]]
    mx = jnp.maximum(jnp.maximum(st[0], st[1]), st[2])
    ex = [jnp.exp2(s - mx) for s in st]
    inv = 1.0 / (ex[0] + ex[1] + ex[2])
    e = e_ref[...]
    w0 = _dot((ex[0] * inv).astype(BF16), e)
    w1 = _dot((ex[1] * inv).astype(BF16), e)
    w2 = 1.0 - (w0 + w1)
    merged = (w0.astype(BF16) * o0_ref[0] + w1.astype(BF16) * o1_ref[0]
              + w2.astype(BF16) * o2_ref[0])
    act = merged * _silu(gate_ref[0].astype(F32)).astype(BF16)
    y = _dot(act, w_out_ref[...])
    out_ref[0] = _deepnorm_ln(h_ref[0], y, g_ref[...], b_ref[...])


def _merge_layer(outs, stats, gate, h, w_out, ln_g, ln_b, *, tile=1024):
    B, S, D = h.shape
    e = _head_expand_matrix()
    row = lambda width: pl.BlockSpec((1, tile, width), lambda b, s: (b, s, 0))
    return pl.pallas_call(
        _merge_kernel,
        grid=(B, S // tile),
        in_specs=[row(D_B)] * 3 + [row(LANES)] * 3 + [row(D_B), row(D)] + [
            _const_spec(e.shape), _const_spec(w_out.shape),
            _const_spec(ln_g.shape), _const_spec(ln_b.shape)],
        out_specs=row(D),
        out_shape=jax.ShapeDtypeStruct((B, S, D), F32),
        compiler_params=pltpu.CompilerParams(
            dimension_semantics=("arbitrary", "arbitrary"),
            vmem_limit_bytes=VMEM_LIMIT_BYTES),
        name="merge_layer",
    )(*outs, *stats, gate, h, e, w_out, ln_g, ln_b)


def kernel(x, w_in_a, w_grp_a, scale_a, w_out_a, w_kv, w_in_b, w_out_b, ln_g, ln_b):
    n_q = N_GROUPS * D_B
    B, S, _ = x.shape
    dils = [d for _, d in ATTN_PAIRS]
    q_scale = jnp.concatenate([jnp.full((n_q,), HEAD_DIM ** -0.5 * LOG2_E, F32), jnp.ones((D_B,), F32)])
    kv_plan = ([(grp * D_B, d) for grp, d in enumerate(dils)]
               + [(n_q + grp * D_B, d) for grp, d in enumerate(dils)])
    q_plan = [(grp * D_B, d) for grp, d in enumerate(dils)] + [(n_q, 1)]
    h = x
    ks = vs = None
    for l in range(DEPTH):
        g = ln_g[l][None, :]
        b = ln_b[l][None, :]
        if l < N_A_LAYERS:
            h = _pool_layer(h, w_in_a[l].astype(BF16), w_grp_a[l].astype(BF16),
                            scale_a[l][None, :], w_out_a[l].astype(BF16), g, b)
            if l == N_A_LAYERS - 1:
                kv = _project(h, w_kv.astype(BF16), kv_plan, [BF16] * (2 * N_GROUPS))
                ks, vs = kv[:N_GROUPS], kv[N_GROUPS:]
        else:
            j = l - N_A_LAYERS
            *qs, gate = _project(h, (w_in_b[j] * q_scale).astype(BF16), q_plan,
                                 [BF16] * (N_GROUPS + 1))
            outs, stats = zip(*[_attention_group(qs[grp], ks[grp], vs[grp], grp)
                                for grp in range(N_GROUPS)])
            h = _merge_layer(outs, stats, gate.reshape(B, S, D_B), h, w_out_b[j].astype(BF16), g, b)
    return h
```

```python
import functools

import jax
import jax.numpy as jnp
import numpy as np
from jax import lax
from jax.experimental import pallas as pl
from jax.experimental.pallas import tpu as pltpu

D_MODEL = 1024
DEPTH = 4
N_A_LAYERS = DEPTH // 2
D_A = 2 * D_MODEL
POOL_WINDOWS = (2, 4, 8, 16)
N_POOL_GROUPS = len(POOL_WINDOWS)
POOL_GW = D_A // N_POOL_GROUPS
ATTN_PAIRS = ((128, 1), (512, 4), (2048, 16))
N_GROUPS = len(ATTN_PAIRS)
N_HEADS = 16
HEAD_DIM = D_MODEL // N_HEADS
D_B = N_HEADS * HEAD_DIM
BLK = 128
DN_ALPHA = (2 * DEPTH) ** 0.25
LN_EPS = 1e-5
NEG_INF = -1e30
LOG2_E = 1.4426950408889634

LANES = 128
N_PAIRS = N_HEADS // 2
SUBLANES = 8
POOL_HALO = SUBLANES * N_POOL_GROUPS
VMEM_LIMIT_BYTES = 56 * 1024 * 1024
ATTN_UNROLL = 8

F32 = jnp.float32
BF16 = jnp.bfloat16


def _dot(a, b):
    return jnp.dot(a, b, preferred_element_type=F32)


def _silu(x):
    hx = 0.5 * x
    return hx + hx * jnp.tanh(hx)


def _deepnorm_ln(h, y, g, b):
    z = DN_ALPHA * h + y
    mu = jnp.mean(z, axis=-1, keepdims=True)
    zc = z - mu
    var = jnp.mean(zc * zc, axis=-1, keepdims=True)
    return zc * lax.rsqrt(var + LN_EPS) * g + b


def _const_spec(shape):
    nd = len(shape)
    return pl.BlockSpec(shape, lambda *_: (0,) * nd, pipeline_mode=pl.Buffered(1))


def _fold_kernel(w_u_ref, w_grp_ref, scale_ref, o_ref):
    o_ref[...] = (_dot(w_u_ref[...], w_grp_ref[0]) * scale_ref[...]).astype(BF16)


def _fold_pool_weights(w_in, w_grp, scale):
    D = w_in.shape[0]
    return pl.pallas_call(
        _fold_kernel,
        grid=(N_POOL_GROUPS,),
        in_specs=[
            pl.BlockSpec((D, POOL_GW), lambda g: (0, g)),
            pl.BlockSpec((1, POOL_GW, POOL_GW), lambda g: (g, 0, 0)),
            pl.BlockSpec((1, POOL_GW), lambda g: (0, g)),
        ],
        out_specs=pl.BlockSpec((D, POOL_GW), lambda g: (0, g)),
        out_shape=jax.ShapeDtypeStruct((D, D_A), BF16),
        compiler_params=pltpu.CompilerParams(dimension_semantics=("arbitrary",)),
        name="fold_pool_weights",
    )(w_in, w_grp, scale)


def _pool_layer_kernel(h_ref, w_mix_ref, w_gate_ref, w_out_ref, g_ref, b_ref,
                       o_ref, hext_ref, *sum_refs, tile):
    s = pl.program_id(1)
    halo, end = POOL_HALO, POOL_HALO + tile

    @pl.when(s == 0)
    def _():
        hext_ref[0:halo, :] = jnp.zeros((halo, D_MODEL), F32)

    h = h_ref[0]
    hb = h.astype(BF16)
    gates = [_dot(hb, w_gate_ref[:, g * POOL_GW:(g + 1) * POOL_GW]) for g in range(N_POOL_GROUPS)]
    hext_ref[halo:end, :] = h
    src, w, lo = hext_ref, 1, 0
    window_sums = []
    for dst in sum_refs:
        lo += SUBLANES
        dst[lo:end, :] = src[lo:end, :] + src[lo - w:end - w, :]
        window_sums.append(dst[halo:end, :])
        src, w = dst, 2 * w
    window_sums.append(src[halo:end, :] + src[halo - w:end - w, :])

    pos = s * tile + lax.broadcasted_iota(jnp.int32, (tile, 1), 0)
    y = jnp.zeros((tile, D_MODEL), F32)
    for g, (w, wsum) in enumerate(zip(POOL_WINDOWS, window_sums)):
        c0 = g * POOL_GW
        inv_cnt = 1.0 / jnp.minimum(pos + 1, w).astype(F32)
        lhs = (wsum * inv_cnt - h).astype(BF16)
        mixed = _dot(lhs, w_mix_ref[:, c0:c0 + POOL_GW])
        act = mixed * _silu(gates[g])
        y = y + _dot(act.astype(BF16), w_out_ref[c0:c0 + POOL_GW, :])
    hext_ref[0:halo, :] = hext_ref[tile:end, :]
    o_ref[0] = _deepnorm_ln(h, y, g_ref[...], b_ref[...])


def _pool_layer(h, w_in, w_grp, scale, w_out, ln_g, ln_b, *, tile=512):
    B, S, D = h.shape
    w_mix = _fold_pool_weights(w_in, w_grp, scale)
    kern = functools.partial(_pool_layer_kernel, tile=tile)
    return pl.pallas_call(
        kern,
        grid=(B, S // tile),
        in_specs=[
            pl.BlockSpec((1, tile, D), lambda b, s: (b, s, 0)),
            _const_spec(w_mix.shape),
            pl.BlockSpec((D, D_A), lambda b, s: (0, 1), pipeline_mode=pl.Buffered(1)),
            _const_spec(w_out.shape),
            _const_spec(ln_g.shape),
            _const_spec(ln_b.shape),
        ],
        out_specs=pl.BlockSpec((1, tile, D), lambda b, s: (b, s, 0)),
        out_shape=jax.ShapeDtypeStruct((B, S, D), F32),
        scratch_shapes=[pltpu.VMEM((POOL_HALO + tile, D), F32)] * N_POOL_GROUPS,
        compiler_params=pltpu.CompilerParams(
            dimension_semantics=("arbitrary", "arbitrary"),
            vmem_limit_bytes=VMEM_LIMIT_BYTES),
        name="pool_layer",
    )(h, w_mix, w_in, w_out, ln_g, ln_b)


def _proj_kernel(x_ref, w_ref, *refs, plan, tile):
    out_refs, slab_ref = refs[:len(plan)], refs[len(plan)]
    x = x_ref[0]
    lhs = {1: x.astype(BF16)}
    dils = sorted({d for _, d in plan if d > 1})
    if dils:
        for j in range(D_MODEL // LANES):
            slab_ref[j] = x[:, j * LANES:(j + 1) * LANES]
    for d in dils:
        planes = [
            jnp.concatenate([slab_ref[j, pl.ds(r, tile // d, stride=d), :]
                             for j in range(D_MODEL // LANES)], axis=1)
            for r in range(d)]
        lhs[d] = jnp.concatenate(planes, axis=0).astype(BF16)
    for (c0, d), o_ref in zip(plan, out_refs):
        res = _dot(lhs[d], w_ref[:, c0:c0 + D_B]).astype(o_ref.dtype)
        rows = tile // d
        for r in range(d):
            o_ref[0, r] = res[r * rows:(r + 1) * rows]


def _project(x, w, plan, dtypes, *, tile=512):
    B, S, D = x.shape
    out_shape = [jax.ShapeDtypeStruct((B, d, S // d, D_B), dt) for (_, d), dt in zip(plan, dtypes)]
    out_specs = [pl.BlockSpec((1, d, tile // d, D_B), lambda b, s: (b, 0, s, 0)) for _, d in plan]
    kern = functools.partial(_proj_kernel, plan=tuple(plan), tile=tile)
    return pl.pallas_call(
        kern,
        grid=(B, S // tile),
        in_specs=[pl.BlockSpec((1, tile, D), lambda b, s: (b, s, 0)), _const_spec(w.shape)],
        out_specs=out_specs,
        out_shape=out_shape,
        scratch_shapes=[pltpu.VMEM((D // LANES, tile, LANES), F32)],
        compiler_params=pltpu.CompilerParams(
            dimension_semantics=("arbitrary", "arbitrary"),
            vmem_limit_bytes=VMEM_LIMIT_BYTES),
        name="project",
    )(x, w)


def _attn_bias(group, dilation):
    n = N_GROUPS * N_HEADS
    i = np.arange(1, n + 1, dtype=np.float32)
    slopes = np.exp2(np.float32(-8.0) * i / np.float32(n)).reshape(N_GROUPS, N_HEADS)[group]
    a = np.arange(BLK)[:, None]
    c = np.arange(2 * BLK)[None, :]
    rel = BLK + a - c
    win_sub = ATTN_PAIRS[group][0] // dilation
    valid = (rel >= 0) & (rel <= win_sub)
    bias = (-slopes[:, None, None] * (dilation * rel).astype(np.float32)[None]
            * np.float32(LOG2_E)).astype(np.float32)
    later = np.where(valid[None], bias, np.float32(NEG_INF))
    first = np.where((valid & (c >= BLK))[None], bias, np.float32(NEG_INF))
    table = np.stack([later, first], axis=0).reshape(2, N_PAIRS, 2 * BLK, 2 * BLK)
    return jnp.asarray(table, dtype=F32)


def _attn_kernel(q_ref, kp_ref, kc_ref, vp_ref, vc_ref, bias_ref, o_ref, st_ref, *scratch,
                 dilation, res_per_step, blk_per_step):
    n = pl.program_id(1)
    rg = pl.program_id(2)
    lane = lax.broadcasted_iota(jnp.int32, (1, LANES), 1)
    lo = lane < HEAD_DIM
    ones = jnp.ones((2 * BLK, LANES), BF16)

    def block(rr, j, key_window, bsel):
        row0 = j * BLK if isinstance(j, int) else pl.multiple_of(j * BLK, BLK)
        out0 = row0 * dilation + rg * res_per_step + rr
        stats = jnp.zeros((BLK, LANES), F32)
        for p in range(N_PAIRS):
            cols = slice(p * LANES, (p + 1) * LANES)
            qp = q_ref[0, rr, pl.ds(row0, BLK), cols]
            kp = key_window(kp_ref, kc_ref, cols)
            vp = key_window(vp_ref, vc_ref, cols)
            zero = jnp.zeros_like(qp)
            q2 = jnp.concatenate([jnp.where(lo, qp, zero), jnp.where(lo, zero, qp)], axis=0)
            sc = lax.dot_general(q2, kp, (((1,), (1,)), ((), ())), preferred_element_type=F32)
            sc = sc + bias_ref[bsel, p]
            m = jnp.max(sc, axis=-1, keepdims=True)
            pe = jnp.exp2((sc - m).astype(BF16))
            acc = _dot(pe, jnp.concatenate([vp, ones], axis=1))
            pick = lambda x: jnp.where(lo, x[:BLK], x[BLK:])
            denom = pick(acc[:, LANES:])
            m_pair = pick(m)
            o = pick(acc[:, :LANES]) / denom
            if dilation == 1:
                o_ref[0, pl.ds(row0, BLK), cols] = o.astype(BF16)
            else:
                scratch[0][p, pl.ds(out0, BLK, stride=dilation), :] = o
            lse = m_pair + jnp.log2(denom)
            stats = jnp.where((lane & (HEAD_DIM - 1)) == p, lse, stats)
        if dilation == 1:
            st_ref[0, pl.ds(row0, BLK), :] = stats
        else:
            st_ref[0, pl.ds(out0, BLK, stride=dilation), :] = stats

    first_tile = (n == 0).astype(jnp.int32)

    def first_block(rr, carry):
        window = lambda prev, cur, cols: jnp.concatenate(
            [prev[0, rr, :, cols], cur[0, rr, 0:BLK, cols]], axis=0)
        block(rr, 0, window, first_tile)
        return carry

    lax.fori_loop(0, res_per_step, first_block, 0, unroll=min(ATTN_UNROLL, res_per_step))

    rest = blk_per_step - 1
    if rest:
        def later_block(i, carry):
            rr, j = i // rest, 1 + i % rest
            k0 = pl.multiple_of((j - 1) * BLK, BLK)
            window = lambda prev, cur, cols: cur[0, rr, pl.ds(k0, 2 * BLK), cols]
            block(rr, j, window, 0)
            return carry

        lax.fori_loop(0, res_per_step * rest, later_block, 0, unroll=ATTN_UNROLL)

    if dilation > 1:
        @pl.when(rg == dilation // res_per_step - 1)
        def _():
            for p in range(N_PAIRS):
                o_ref[0, :, p * LANES:(p + 1) * LANES] = scratch[0][p].astype(BF16)


ATTN_STEP_SHAPE = {1: (1, 8), 4: (4, 2), 16: (8, 1)}


def _attention_group(q, k, v, group):
    _, d = ATTN_PAIRS[group]
    B, _, L, _ = q.shape
    S = L * d
    rb, qb = ATTN_STEP_SHAPE[d]
    rows = qb * BLK
    bias = _attn_bias(group, d)
    cur_spec = pl.BlockSpec((1, rb, rows, D_B), lambda b, n, rg: (b, rg, n, 0))
    prev_spec = pl.BlockSpec((1, rb, BLK, D_B), lambda b, n, rg: (b, rg, jnp.maximum(n * qb - 1, 0), 0))
    kern = functools.partial(_attn_kernel, dilation=d, res_per_step=rb, blk_per_step=qb)
    return pl.pallas_call(
        kern,
        grid=(B, L // rows, d // rb),
        in_specs=[cur_spec, prev_spec, cur_spec, prev_spec, cur_spec, _const_spec(bias.shape)],
        out_specs=[
            pl.BlockSpec((1, rows * d, D_B), lambda b, n, rg: (b, n, 0)),
            pl.BlockSpec((1, rows * d, LANES), lambda b, n, rg: (b, n, 0)),
        ],
        out_shape=[
            jax.ShapeDtypeStruct((B, S, D_B), BF16),
            jax.ShapeDtypeStruct((B, S, LANES), F32),
        ],
        scratch_shapes=[pltpu.VMEM((N_PAIRS, rows * d, LANES), F32)] if d > 1 else [],
        compiler_params=pltpu.CompilerParams(
            dimension_semantics=("arbitrary", "arbitrary", "arbitrary"),
            vmem_limit_bytes=VMEM_LIMIT_BYTES),
        name=f"attn_g{group}",
    )(q, k, k, v, v, bias)


def _head_expand_matrix():
    row = np.arange(LANES)[:, None]
    col = np.arange(D_B)[None, :]
    head = col // HEAD_DIM
    src = (head // 2) + (head % 2) * HEAD_DIM
    return jnp.asarray((row == src).astype(np.float32), dtype=BF16)


def _merge_kernel(o0_ref, o1_ref, o2_ref, s0_ref, s1_ref, s2_ref, gate_ref, h_ref,
                  e_ref, w_out_ref, g_ref, b_ref, out_ref):
    st = [s0_ref[0], s1_ref[0], s2_ref[0]]
    mx = jnp.maximum(jnp.maximum(st[0], st[1]), st[2])
    ex = [jnp.exp2(s - mx) for s in st]
    inv = 1.0 / (ex[0] + ex[1] + ex[2])
    e = e_ref[...]
    w0 = _dot((ex[0] * inv).astype(BF16), e)
    w1 = _dot((ex[1] * inv).astype(BF16), e)
    w2 = 1.0 - (w0 + w1)
    merged = (w0.astype(BF16) * o0_ref[0] + w1.astype(BF16) * o1_ref[0]
              + w2.astype(BF16) * o2_ref[0])
    act = merged * _silu(gate_ref[0].astype(F32)).astype(BF16)
    y = _dot(act, w_out_ref[...])
    out_ref[0] = _deepnorm_ln(h_ref[0], y, g_ref[...], b_ref[...])


def _merge_layer(outs, stats, gate, h, w_out, ln_g, ln_b, *, tile=1024):
    B, S, D = h.shape
    e = _head_expand_matrix()
    row = lambda width: pl.BlockSpec((1, tile, width), lambda b, s: (b, s, 0))
    return pl.pallas_call(
        _merge_kernel,
        grid=(B, S // tile),
        in_specs=[row(D_B)] * 3 + [row(LANES)] * 3 + [row(D_B), row(D)] + [
            _const_spec(e.shape), _const_spec(w_out.shape),
            _const_spec(ln_g.shape), _const_spec(ln_b.shape)],
        out_specs=row(D),
        out_shape=jax.ShapeDtypeStruct((B, S, D), F32),
        compiler_params=pltpu.CompilerParams(
            dimension_semantics=("arbitrary", "arbitrary"),
            vmem_limit_bytes=VMEM_LIMIT_BYTES),
        name="merge_layer",
    )(*outs, *stats, gate, h, e, w_out, ln_g, ln_b)


def kernel(x, w_in_a, w_grp_a, scale_a, w_out_a, w_kv, w_in_b, w_out_b, ln_g, ln_b):
    n_q = N_GROUPS * D_B
    B, S, _ = x.shape
    dils = [d for _, d in ATTN_PAIRS]
    q_scale = jnp.concatenate([jnp.full((n_q,), HEAD_DIM ** -0.5 * LOG2_E, F32), jnp.ones((D_B,), F32)])
    kv_plan = ([(grp * D_B, d) for grp, d in enumerate(dils)]
               + [(n_q + grp * D_B, d) for grp, d in enumerate(dils)])
    q_plan = [(grp * D_B, d) for grp, d in enumerate(dils)] + [(n_q, 1)]
    w_q = [(w_in_b[j] * q_scale).astype(BF16) for j in range(DEPTH - N_A_LAYERS)]
    h = x
    ks = vs = qs = gate = None
    for l in range(DEPTH):
        g = ln_g[l][None, :]
        b = ln_b[l][None, :]
        if l < N_A_LAYERS:
            h = _pool_layer(h, w_in_a[l].astype(BF16), w_grp_a[l].astype(BF16),
                            scale_a[l][None, :], w_out_a[l].astype(BF16), g, b)
            if l == N_A_LAYERS - 1:
                n_kv = w_kv.shape[1]
                w_cat = jnp.concatenate([w_kv.astype(BF16), w_q[0]], axis=1)
                plan = kv_plan + [(n_kv + c0, d) for c0, d in q_plan]
                outs = _project(h, w_cat, plan, [BF16] * len(plan))
                ks, vs = outs[:N_GROUPS], outs[N_GROUPS:2 * N_GROUPS]
                *qs, gate = outs[2 * N_GROUPS:]
        else:
            j = l - N_A_LAYERS
            if j > 0:
                *qs, gate = _project(h, w_q[j], q_plan, [BF16] * (N_GROUPS + 1))
            outs, stats = zip(*[_attention_group(qs[grp], ks[grp], vs[grp], grp)
                                for grp in range(N_GROUPS)])
            h = _merge_layer(outs, stats, gate.reshape(B, S, D_B), h, w_out_b[j].astype(BF16), g, b)
    return h
```

```python
import functools

import jax
import jax.numpy as jnp
import numpy as np
from jax import lax
from jax.experimental import pallas as pl
from jax.experimental.pallas import tpu as pltpu

D_MODEL = 1024
DEPTH = 4
N_A_LAYERS = DEPTH // 2
D_A = 2 * D_MODEL
POOL_WINDOWS = (2, 4, 8, 16)
N_POOL_GROUPS = len(POOL_WINDOWS)
POOL_GW = D_A // N_POOL_GROUPS
ATTN_PAIRS = ((128, 1), (512, 4), (2048, 16))
N_GROUPS = len(ATTN_PAIRS)
N_HEADS = 16
HEAD_DIM = D_MODEL // N_HEADS
D_B = N_HEADS * HEAD_DIM
BLK = 128
DN_ALPHA = (2 * DEPTH) ** 0.25
LN_EPS = 1e-5
NEG_INF = -1e30
LOG2_E = 1.4426950408889634

LANES = 128
N_PAIRS = N_HEADS // 2
SUBLANES = 8
POOL_HALO = SUBLANES * N_POOL_GROUPS
VMEM_LIMIT_BYTES = 56 * 1024 * 1024
ATTN_UNROLL = 8

F32 = jnp.float32
BF16 = jnp.bfloat16


def _dot(a, b):
    return jnp.dot(a, b, preferred_element_type=F32)


def _silu(x):
    hx = 0.5 * x
    return hx + hx * jnp.tanh(hx)


def _deepnorm_ln(h, y, g, b):
    z = DN_ALPHA * h + y
    mu = jnp.mean(z, axis=-1, keepdims=True)
    zc = z - mu
    var = jnp.mean(zc * zc, axis=-1, keepdims=True)
    return zc * lax.rsqrt(var + LN_EPS) * g + b


def _const_spec(shape):
    nd = len(shape)
    return pl.BlockSpec(shape, lambda *_: (0,) * nd, pipeline_mode=pl.Buffered(1))


def _fold_kernel(w_u_ref, w_grp_ref, scale_ref, o_ref):
    o_ref[...] = (_dot(w_u_ref[...], w_grp_ref[0]) * scale_ref[...]).astype(BF16)


def _fold_pool_weights(w_in, w_grp, scale):
    D = w_in.shape[0]
    return pl.pallas_call(
        _fold_kernel,
        grid=(N_POOL_GROUPS,),
        in_specs=[
            pl.BlockSpec((D, POOL_GW), lambda g: (0, g)),
            pl.BlockSpec((1, POOL_GW, POOL_GW), lambda g: (g, 0, 0)),
            pl.BlockSpec((1, POOL_GW), lambda g: (0, g)),
        ],
        out_specs=pl.BlockSpec((D, POOL_GW), lambda g: (0, g)),
        out_shape=jax.ShapeDtypeStruct((D, D_A), BF16),
        compiler_params=pltpu.CompilerParams(dimension_semantics=("arbitrary",)),
        name="fold_pool_weights",
    )(w_in, w_grp, scale)


def _pool_layer_kernel(h_ref, w_mix_ref, w_gate_ref, w_out_ref, g_ref, b_ref,
                       o_ref, hext_ref, *sum_refs, tile):
    s = pl.program_id(1)
    halo, end = POOL_HALO, POOL_HALO + tile

    @pl.when(s == 0)
    def _():
        hext_ref[0:halo, :] = jnp.zeros((halo, D_MODEL), F32)

    h = h_ref[0]
    hb = h.astype(BF16)
    gates = [_dot(hb, w_gate_ref[:, g * POOL_GW:(g + 1) * POOL_GW]) for g in range(N_POOL_GROUPS)]
    hext_ref[halo:end, :] = h
    src, w, lo = hext_ref, 1, 0
    window_sums = []
    for dst in sum_refs:
        lo += SUBLANES
        dst[lo:end, :] = src[lo:end, :] + src[lo - w:end - w, :]
        window_sums.append(dst[halo:end, :])
        src, w = dst, 2 * w
    window_sums.append(src[halo:end, :] + src[halo - w:end - w, :])

    pos = s * tile + lax.broadcasted_iota(jnp.int32, (tile, 1), 0)
    y = jnp.zeros((tile, D_MODEL), F32)
    for g, (w, wsum) in enumerate(zip(POOL_WINDOWS, window_sums)):
        c0 = g * POOL_GW
        inv_cnt = 1.0 / jnp.minimum(pos + 1, w).astype(F32)
        lhs = (wsum * inv_cnt - h).astype(BF16)
        mixed = _dot(lhs, w_mix_ref[:, c0:c0 + POOL_GW])
        act = mixed * _silu(gates[g])
        y = y + _dot(act.astype(BF16), w_out_ref[c0:c0 + POOL_GW, :])
    hext_ref[0:halo, :] = hext_ref[tile:end, :]
    o_ref[0] = _deepnorm_ln(h, y, g_ref[...], b_ref[...])


def _pool_layer(h, w_in, w_grp, scale, w_out, ln_g, ln_b, *, tile=512):
    B, S, D = h.shape
    w_mix = _fold_pool_weights(w_in, w_grp, scale)
    kern = functools.partial(_pool_layer_kernel, tile=tile)
    return pl.pallas_call(
        kern,
        grid=(B, S // tile),
        in_specs=[
            pl.BlockSpec((1, tile, D), lambda b, s: (b, s, 0)),
            _const_spec(w_mix.shape),
            pl.BlockSpec((D, D_A), lambda b, s: (0, 1), pipeline_mode=pl.Buffered(1)),
            _const_spec(w_out.shape),
            _const_spec(ln_g.shape),
            _const_spec(ln_b.shape),
        ],
        out_specs=pl.BlockSpec((1, tile, D), lambda b, s: (b, s, 0)),
        out_shape=jax.ShapeDtypeStruct((B, S, D), F32),
        scratch_shapes=[pltpu.VMEM((POOL_HALO + tile, D), F32)] * N_POOL_GROUPS,
        compiler_params=pltpu.CompilerParams(
            dimension_semantics=("arbitrary", "arbitrary"),
            vmem_limit_bytes=VMEM_LIMIT_BYTES),
        name="pool_layer",
    )(h, w_mix, w_in, w_out, ln_g, ln_b)


def _proj_kernel(x_ref, *refs, plan, tile, n_weights):
    w_refs = refs[:n_weights]
    out_refs, slab_ref = refs[n_weights:n_weights + len(plan)], refs[n_weights + len(plan)]
    x = x_ref[0]
    lhs = {1: x.astype(BF16)}
    dils = sorted({d for _, _, d, _ in plan if d > 1})
    if dils:
        for j in range(D_MODEL // LANES):
            slab_ref[j] = x[:, j * LANES:(j + 1) * LANES]
    for d in dils:
        planes = [
            jnp.concatenate([slab_ref[j, pl.ds(r, tile // d, stride=d), :]
                             for j in range(D_MODEL // LANES)], axis=1)
            for r in range(d)]
        lhs[d] = jnp.concatenate(planes, axis=0).astype(BF16)
    for (wi, c0, d, gated), o_ref in zip(plan, out_refs):
        res = _dot(lhs[d], w_refs[wi][:, c0:c0 + D_B])
        res = (_silu(res) if gated else res).astype(o_ref.dtype)
        rows = tile // d
        for r in range(d):
            o_ref[0, r] = res[r * rows:(r + 1) * rows]


def _project(x, weights, plan, *, tile=512):
    B, S, D = x.shape
    out_shape = [jax.ShapeDtypeStruct((B, d, S // d, D_B), BF16) for _, _, d, _ in plan]
    out_specs = [pl.BlockSpec((1, d, tile // d, D_B), lambda b, s: (b, 0, s, 0)) for _, _, d, _ in plan]
    kern = functools.partial(_proj_kernel, plan=tuple(plan), tile=tile, n_weights=len(weights))
    return pl.pallas_call(
        kern,
        grid=(B, S // tile),
        in_specs=[pl.BlockSpec((1, tile, D), lambda b, s: (b, s, 0))]
                 + [_const_spec(w.shape) for w in weights],
        out_specs=out_specs,
        out_shape=out_shape,
        scratch_shapes=[pltpu.VMEM((D // LANES, tile, LANES), F32)],
        compiler_params=pltpu.CompilerParams(
            dimension_semantics=("arbitrary", "arbitrary"),
            vmem_limit_bytes=VMEM_LIMIT_BYTES),
        name="project",
    )(x, *weights)


def _attn_bias(group, dilation):
    n = N_GROUPS * N_HEADS
    i = np.arange(1, n + 1, dtype=np.float32)
    slopes = np.exp2(np.float32(-8.0) * i / np.float32(n)).reshape(N_GROUPS, N_HEADS)[group]
    a = np.arange(BLK)[:, None]
    c = np.arange(2 * BLK)[None, :]
    rel = BLK + a - c
    win_sub = ATTN_PAIRS[group][0] // dilation
    valid = (rel >= 0) & (rel <= win_sub)
    bias = (-slopes[:, None, None] * (dilation * rel).astype(np.float32)[None]
            * np.float32(LOG2_E)).astype(np.float32)
    later = np.where(valid[None], bias, np.float32(NEG_INF))
    first = np.where((valid & (c >= BLK))[None], bias, np.float32(NEG_INF))
    table = np.stack([later, first], axis=0).reshape(2, N_PAIRS, 2 * BLK, 2 * BLK)
    return jnp.asarray(table, dtype=F32)


def _attn_kernel(q_ref, kp_ref, kc_ref, vp_ref, vc_ref, bias_ref, o_ref, st_ref, *scratch,
                 dilation, res_per_step, blk_per_step):
    n = pl.program_id(1)
    rg = pl.program_id(2)
    lane = lax.broadcasted_iota(jnp.int32, (1, LANES), 1)
    lo = lane < HEAD_DIM
    ones = jnp.ones((2 * BLK, LANES), BF16)

    def block(rr, j, key_window, bias_tile):
        row0 = j * BLK if isinstance(j, int) else pl.multiple_of(j * BLK, BLK)
        out0 = row0 * dilation + rg * res_per_step + rr
        stats = jnp.zeros((BLK, LANES), F32)
        for p in range(N_PAIRS):
            cols = slice(p * LANES, (p + 1) * LANES)
            qp = q_ref[0, rr, pl.ds(row0, BLK), cols]
            kp = key_window(kp_ref, kc_ref, cols)
            vp = key_window(vp_ref, vc_ref, cols)
            zero = jnp.zeros_like(qp)
            q2 = jnp.concatenate([jnp.where(lo, qp, zero), jnp.where(lo, zero, qp)], axis=0)
            sc = lax.dot_general(q2, kp, (((1,), (1,)), ((), ())), preferred_element_type=F32)
            sc = sc + bias_tile(p)
            m = jnp.max(sc, axis=-1, keepdims=True)
            pe = jnp.exp2((sc - m).astype(BF16))
            acc = _dot(pe, jnp.concatenate([vp, ones], axis=1))
            pick = lambda x: jnp.where(lo, x[:BLK], x[BLK:])
            denom = pick(acc[:, LANES:])
            m_pair = pick(m)
            o = pick(acc[:, :LANES]) / denom
            if dilation == 1:
                o_ref[0, pl.ds(row0, BLK), cols] = o.astype(BF16)
            else:
                scratch[0][p, pl.ds(out0, BLK, stride=dilation), :] = o
            lse = m_pair + jnp.log2(denom)
            stats = jnp.where((lane & (HEAD_DIM - 1)) == p, lse, stats)
        if dilation == 1:
            st_ref[0, pl.ds(row0, BLK), :] = stats
        else:
            st_ref[0, pl.ds(out0, BLK, stride=dilation), :] = stats

    first_tile = (n == 0).astype(jnp.int32)

    def first_block(rr, carry):
        window = lambda prev, cur, cols: jnp.concatenate(
            [prev[0, rr, :, cols], cur[0, rr, 0:BLK, cols]], axis=0)
        block(rr, 0, window, lambda p: bias_ref[first_tile, p])
        return carry

    lax.fori_loop(0, res_per_step, first_block, 0, unroll=min(ATTN_UNROLL, res_per_step))

    rest = blk_per_step - 1
    if rest:
        def later_block(i, carry):
            rr, j = i // rest, 1 + i % rest
            k0 = pl.multiple_of((j - 1) * BLK, BLK)
            window = lambda prev, cur, cols: cur[0, rr, pl.ds(k0, 2 * BLK), cols]
            block(rr, j, window, lambda p: bias_ref[0, p])
            return carry

        lax.fori_loop(0, res_per_step * rest, later_block, 0, unroll=ATTN_UNROLL)

    if dilation > 1:
        @pl.when(rg == dilation // res_per_step - 1)
        def _():
            for p in range(N_PAIRS):
                o_ref[0, :, p * LANES:(p + 1) * LANES] = scratch[0][p].astype(BF16)


ATTN_STEP_SHAPE = {1: (1, 8), 4: (4, 2), 16: (8, 1)}


def _attention_group(q, k, v, group):
    _, d = ATTN_PAIRS[group]
    B, _, L, _ = q.shape
    S = L * d
    rb, qb = ATTN_STEP_SHAPE[d]
    rows = qb * BLK
    bias = _attn_bias(group, d)
    cur_spec = pl.BlockSpec((1, rb, rows, D_B), lambda b, n, rg: (b, rg, n, 0))
    prev_spec = pl.BlockSpec((1, rb, BLK, D_B), lambda b, n, rg: (b, rg, jnp.maximum(n * qb - 1, 0), 0))
    kern = functools.partial(_attn_kernel, dilation=d, res_per_step=rb, blk_per_step=qb)
    return pl.pallas_call(
        kern,
        grid=(B, L // rows, d // rb),
        in_specs=[cur_spec, prev_spec, cur_spec, prev_spec, cur_spec, _const_spec(bias.shape)],
        out_specs=[
            pl.BlockSpec((1, rows * d, D_B), lambda b, n, rg: (b, n, 0)),
            pl.BlockSpec((1, rows * d, LANES), lambda b, n, rg: (b, n, 0)),
        ],
        out_shape=[
            jax.ShapeDtypeStruct((B, S, D_B), BF16),
            jax.ShapeDtypeStruct((B, S, LANES), F32),
        ],
        scratch_shapes=[pltpu.VMEM((N_PAIRS, rows * d, LANES), F32)] if d > 1 else [],
        compiler_params=pltpu.CompilerParams(
            dimension_semantics=("arbitrary", "arbitrary", "arbitrary"),
            vmem_limit_bytes=VMEM_LIMIT_BYTES),
        name=f"attn_g{group}",
    )(q, k, k, v, v, bias)


def _head_expand_matrix():
    row = np.arange(LANES)[:, None]
    col = np.arange(D_B)[None, :]
    head = col // HEAD_DIM
    src = (head // 2) + (head % 2) * HEAD_DIM
    return jnp.asarray((row == src).astype(np.float32), dtype=BF16)


def _merge_kernel(o0_ref, o1_ref, o2_ref, s0_ref, s1_ref, s2_ref, gate_ref, h_ref,
                  e_ref, w_out_ref, g_ref, b_ref, out_ref):
    st = [s0_ref[0], s1_ref[0], s2_ref[0]]
    mx = jnp.maximum(jnp.maximum(st[0], st[1]), st[2])
    ex = [jnp.exp2(s - mx) for s in st]
    inv = 1.0 / (ex[0] + ex[1] + ex[2])
    e = e_ref[...]
    w0 = _dot((ex[0] * inv).astype(BF16), e)
    w1 = _dot((ex[1] * inv).astype(BF16), e)
    w2 = 1.0 - (w0 + w1)
    merged = (w0.astype(BF16) * o0_ref[0] + w1.astype(BF16) * o1_ref[0]
              + w2.astype(BF16) * o2_ref[0])
    act = merged * gate_ref[0]
    y = _dot(act, w_out_ref[...])
    out_ref[0] = _deepnorm_ln(h_ref[0], y, g_ref[...], b_ref[...])


def _merge_layer(outs, stats, gate, h, w_out, ln_g, ln_b, *, tile=1024):
    B, S, D = h.shape
    e = _head_expand_matrix()
    row = lambda width: pl.BlockSpec((1, tile, width), lambda b, s: (b, s, 0))
    return pl.pallas_call(
        _merge_kernel,
        grid=(B, S // tile),
        in_specs=[row(D_B)] * 3 + [row(LANES)] * 3 + [row(D_B), row(D)] + [
            _const_spec(e.shape), _const_spec(w_out.shape),
            _const_spec(ln_g.shape), _const_spec(ln_b.shape)],
        out_specs=row(D),
        out_shape=jax.ShapeDtypeStruct((B, S, D), F32),
        compiler_params=pltpu.CompilerParams(
            dimension_semantics=("arbitrary", "arbitrary"),
            vmem_limit_bytes=VMEM_LIMIT_BYTES),
        name="merge_layer",
    )(*outs, *stats, gate, h, e, w_out, ln_g, ln_b)


def kernel(x, w_in_a, w_grp_a, scale_a, w_out_a, w_kv, w_in_b, w_out_b, ln_g, ln_b):
    n_q = N_GROUPS * D_B
    B, S, _ = x.shape
    dils = [d for _, d in ATTN_PAIRS]
    q_scale = jnp.concatenate([jnp.full((n_q,), HEAD_DIM ** -0.5 * LOG2_E, F32), jnp.ones((D_B,), F32)])
    def kv_plan(wi):
        return ([(wi, grp * D_B, d, False) for grp, d in enumerate(dils)]
                + [(wi, n_q + grp * D_B, d, False) for grp, d in enumerate(dils)])

    def q_plan(wi):
        return [(wi, grp * D_B, d, False) for grp, d in enumerate(dils)] + [(wi, n_q, 1, True)]

    w_q = [(w_in_b[j] * q_scale).astype(BF16) for j in range(DEPTH - N_A_LAYERS)]
    h = x
    ks = vs = qs = gate = None
    for l in range(DEPTH):
        g = ln_g[l][None, :]
        b = ln_b[l][None, :]
        if l < N_A_LAYERS:
            h = _pool_layer(h, w_in_a[l].astype(BF16), w_grp_a[l].astype(BF16),
                            scale_a[l][None, :], w_out_a[l].astype(BF16), g, b)
            if l == N_A_LAYERS - 1:
                outs = _project(h, [w_kv.astype(BF16), w_q[0]], kv_plan(0) + q_plan(1))
                ks, vs = outs[:N_GROUPS], outs[N_GROUPS:2 * N_GROUPS]
                *qs, gate = outs[2 * N_GROUPS:]
        else:
            j = l - N_A_LAYERS
            if j > 0:
                *qs, gate = _project(h, [w_q[j]], q_plan(0))
            outs, stats = zip(*[_attention_group(qs[grp], ks[grp], vs[grp], grp)
                                for grp in range(N_GROUPS)])
            h = _merge_layer(outs, stats, gate.reshape(B, S, D_B), h, w_out_b[j].astype(BF16), g, b)
    return h
```

```python
import functools

import jax
import jax.numpy as jnp
import numpy as np
from jax import lax
from jax.experimental import pallas as pl
from jax.experimental.pallas import tpu as pltpu

D_MODEL = 1024
DEPTH = 4
N_A_LAYERS = DEPTH // 2
D_A = 2 * D_MODEL
POOL_WINDOWS = (2, 4, 8, 16)
N_POOL_GROUPS = len(POOL_WINDOWS)
POOL_GW = D_A // N_POOL_GROUPS
ATTN_PAIRS = ((128, 1), (512, 4), (2048, 16))
N_GROUPS = len(ATTN_PAIRS)
N_HEADS = 16
HEAD_DIM = D_MODEL // N_HEADS
D_B = N_HEADS * HEAD_DIM
BLK = 128
DN_ALPHA = (2 * DEPTH) ** 0.25
LN_EPS = 1e-5
NEG_INF = -1e30
LOG2_E = 1.4426950408889634

LANES = 128
N_PAIRS = N_HEADS // 2
SUBLANES = 8
POOL_HALO = SUBLANES * N_POOL_GROUPS
VMEM_LIMIT_BYTES = 56 * 1024 * 1024
ATTN_UNROLL = 8

F32 = jnp.float32
BF16 = jnp.bfloat16


def _dot(a, b):
    return jnp.dot(a, b, preferred_element_type=F32)


def _silu(x):
    hx = 0.5 * x
    return hx + hx * jnp.tanh(hx)


def _deepnorm_ln(h, y, g, b):
    z = DN_ALPHA * h + y
    mu = jnp.mean(z, axis=-1, keepdims=True)
    zc = z - mu
    var = jnp.mean(zc * zc, axis=-1, keepdims=True)
    return zc * lax.rsqrt(var + LN_EPS) * g + b


def _const_spec(shape):
    nd = len(shape)
    return pl.BlockSpec(shape, lambda *_: (0,) * nd, pipeline_mode=pl.Buffered(1))


def _fold_kernel(w_u_ref, w_grp_ref, scale_ref, o_ref):
    o_ref[...] = (_dot(w_u_ref[...], w_grp_ref[0]) * scale_ref[...]).astype(BF16)


def _fold_pool_weights(w_in, w_grp, scale):
    D = w_in.shape[0]
    return pl.pallas_call(
        _fold_kernel,
        grid=(N_POOL_GROUPS,),
        in_specs=[
            pl.BlockSpec((D, POOL_GW), lambda g: (0, g)),
            pl.BlockSpec((1, POOL_GW, POOL_GW), lambda g: (g, 0, 0)),
            pl.BlockSpec((1, POOL_GW), lambda g: (0, g)),
        ],
        out_specs=pl.BlockSpec((D, POOL_GW), lambda g: (0, g)),
        out_shape=jax.ShapeDtypeStruct((D, D_A), BF16),
        compiler_params=pltpu.CompilerParams(dimension_semantics=("arbitrary",)),
        name="fold_pool_weights",
    )(w_in, w_grp, scale)


def _pool_layer_kernel(h_ref, w_mix_ref, w_gate_ref, w_out_ref, g_ref, b_ref,
                       o_ref, hext_ref, *sum_refs, tile):
    s = pl.program_id(1)
    halo, end = POOL_HALO, POOL_HALO + tile

    @pl.when(s == 0)
    def _():
        hext_ref[0:halo, :] = jnp.zeros((halo, D_MODEL), F32)

    h = h_ref[0]
    hb = h.astype(BF16)
    gates = [_dot(hb, w_gate_ref[:, g * POOL_GW:(g + 1) * POOL_GW]) for g in range(N_POOL_GROUPS)]
    hext_ref[halo:end, :] = h
    src, w, lo = hext_ref, 1, 0
    window_sums = []
    for dst in sum_refs:
        lo += SUBLANES
        dst[lo:end, :] = src[lo:end, :] + src[lo - w:end - w, :]
        window_sums.append(dst[halo:end, :])
        src, w = dst, 2 * w
    window_sums.append(src[halo:end, :] + src[halo - w:end - w, :])

    pos = s * tile + lax.broadcasted_iota(jnp.int32, (tile, 1), 0)
    y = jnp.zeros((tile, D_MODEL), F32)
    for g, (w, wsum) in enumerate(zip(POOL_WINDOWS, window_sums)):
        c0 = g * POOL_GW
        inv_cnt = 1.0 / jnp.minimum(pos + 1, w).astype(F32)
        lhs = (wsum * inv_cnt - h).astype(BF16)
        mixed = _dot(lhs, w_mix_ref[:, c0:c0 + POOL_GW])
        act = mixed * _silu(gates[g])
        y = y + _dot(act.astype(BF16), w_out_ref[c0:c0 + POOL_GW, :])
    hext_ref[0:halo, :] = hext_ref[tile:end, :]
    o_ref[0] = _deepnorm_ln(h, y, g_ref[...], b_ref[...])


def _pool_layer(h, w_in, w_grp, scale, w_out, ln_g, ln_b, *, tile=512):
    B, S, D = h.shape
    w_mix = _fold_pool_weights(w_in, w_grp, scale)
    kern = functools.partial(_pool_layer_kernel, tile=tile)
    return pl.pallas_call(
        kern,
        grid=(B, S // tile),
        in_specs=[
            pl.BlockSpec((1, tile, D), lambda b, s: (b, s, 0)),
            _const_spec(w_mix.shape),
            pl.BlockSpec((D, D_A), lambda b, s: (0, 1), pipeline_mode=pl.Buffered(1)),
            _const_spec(w_out.shape),
            _const_spec(ln_g.shape),
            _const_spec(ln_b.shape),
        ],
        out_specs=pl.BlockSpec((1, tile, D), lambda b, s: (b, s, 0)),
        out_shape=jax.ShapeDtypeStruct((B, S, D), F32),
        scratch_shapes=[pltpu.VMEM((POOL_HALO + tile, D), F32)] * N_POOL_GROUPS,
        compiler_params=pltpu.CompilerParams(
            dimension_semantics=("arbitrary", "arbitrary"),
            vmem_limit_bytes=VMEM_LIMIT_BYTES),
        name="pool_layer",
    )(h, w_mix, w_in, w_out, ln_g, ln_b)


def _proj_kernel(x_ref, *refs, plan, tile, n_weights):
    w_refs = refs[:n_weights]
    out_refs, slab_ref = refs[n_weights:n_weights + len(plan)], refs[n_weights + len(plan)]
    _project_rows(x_ref[0], w_refs, out_refs, slab_ref, plan, tile)


def _project_rows(x, w_refs, out_refs, slab_ref, plan, tile):
    lhs = {1: x.astype(BF16)}
    dils = sorted({d for _, _, d, _ in plan if d > 1})
    if dils:
        for j in range(D_MODEL // LANES):
            slab_ref[j] = x[:, j * LANES:(j + 1) * LANES]
    for d in dils:
        planes = [
            jnp.concatenate([slab_ref[j, pl.ds(r, tile // d, stride=d), :]
                             for j in range(D_MODEL // LANES)], axis=1)
            for r in range(d)]
        lhs[d] = jnp.concatenate(planes, axis=0).astype(BF16)
    for (wi, c0, d, gated), o_ref in zip(plan, out_refs):
        res = _dot(lhs[d], w_refs[wi][:, c0:c0 + D_B])
        res = (_silu(res) if gated else res).astype(o_ref.dtype)
        rows = tile // d
        for r in range(d):
            o_ref[0, r] = res[r * rows:(r + 1) * rows]


def _project(x, weights, plan, *, tile=512):
    B, S, D = x.shape
    out_shape = [jax.ShapeDtypeStruct((B, d, S // d, D_B), BF16) for _, _, d, _ in plan]
    out_specs = [pl.BlockSpec((1, d, tile // d, D_B), lambda b, s: (b, 0, s, 0)) for _, _, d, _ in plan]
    kern = functools.partial(_proj_kernel, plan=tuple(plan), tile=tile, n_weights=len(weights))
    return pl.pallas_call(
        kern,
        grid=(B, S // tile),
        in_specs=[pl.BlockSpec((1, tile, D), lambda b, s: (b, s, 0))]
                 + [_const_spec(w.shape) for w in weights],
        out_specs=out_specs,
        out_shape=out_shape,
        scratch_shapes=[pltpu.VMEM((D // LANES, tile, LANES), F32)],
        compiler_params=pltpu.CompilerParams(
            dimension_semantics=("arbitrary", "arbitrary"),
            vmem_limit_bytes=VMEM_LIMIT_BYTES),
        name="project",
    )(x, *weights)


def _attn_bias(group, dilation):
    n = N_GROUPS * N_HEADS
    i = np.arange(1, n + 1, dtype=np.float32)
    slopes = np.exp2(np.float32(-8.0) * i / np.float32(n)).reshape(N_GROUPS, N_HEADS)[group]
    a = np.arange(BLK)[:, None]
    c = np.arange(2 * BLK)[None, :]
    rel = BLK + a - c
    win_sub = ATTN_PAIRS[group][0] // dilation
    valid = (rel >= 0) & (rel <= win_sub)
    bias = (-slopes[:, None, None] * (dilation * rel).astype(np.float32)[None]
            * np.float32(LOG2_E)).astype(np.float32)
    later = np.where(valid[None], bias, np.float32(NEG_INF))
    first = np.where((valid & (c >= BLK))[None], bias, np.float32(NEG_INF))
    table = np.stack([later, first], axis=0).reshape(2, N_PAIRS, 2 * BLK, 2 * BLK)
    return jnp.asarray(table, dtype=F32)


def _attn_kernel(q_ref, kp_ref, kc_ref, vp_ref, vc_ref, bias_ref, o_ref, st_ref, *scratch,
                 dilation, res_per_step, blk_per_step):
    n = pl.program_id(1)
    rg = pl.program_id(2)
    lane = lax.broadcasted_iota(jnp.int32, (1, LANES), 1)
    lo = lane < HEAD_DIM
    ones = jnp.ones((2 * BLK, LANES), BF16)

    def block(rr, j, key_window, bias_tile):
        row0 = j * BLK if isinstance(j, int) else pl.multiple_of(j * BLK, BLK)
        out0 = row0 * dilation + rg * res_per_step + rr
        stats = jnp.zeros((BLK, LANES), F32)
        for p in range(N_PAIRS):
            cols = slice(p * LANES, (p + 1) * LANES)
            qp = q_ref[0, rr, pl.ds(row0, BLK), cols]
            kp = key_window(kp_ref, kc_ref, cols)
            vp = key_window(vp_ref, vc_ref, cols)
            zero = jnp.zeros_like(qp)
            q2 = jnp.concatenate([jnp.where(lo, qp, zero), jnp.where(lo, zero, qp)], axis=0)
            sc = lax.dot_general(q2, kp, (((1,), (1,)), ((), ())), preferred_element_type=F32)
            sc = sc + bias_tile(p)
            m = jnp.max(sc, axis=-1, keepdims=True)
            pe = jnp.exp2((sc - m).astype(BF16))
            acc = _dot(pe, jnp.concatenate([vp, ones], axis=1))
            pick = lambda x: jnp.where(lo, x[:BLK], x[BLK:])
            denom = pick(acc[:, LANES:])
            m_pair = pick(m)
            o = pick(acc[:, :LANES]) / denom
            if dilation == 1:
                o_ref[0, pl.ds(row0, BLK), cols] = o.astype(BF16)
            else:
                scratch[0][p, pl.ds(out0, BLK, stride=dilation), :] = o
            lse = m_pair + jnp.log2(denom)
            stats = jnp.where((lane & (HEAD_DIM - 1)) == p, lse, stats)
        if dilation == 1:
            st_ref[0, pl.ds(row0, BLK), :] = stats
        else:
            st_ref[0, pl.ds(out0, BLK, stride=dilation), :] = stats

    first_tile = (n == 0).astype(jnp.int32)

    def first_block(rr, carry):
        window = lambda prev, cur, cols: jnp.concatenate(
            [prev[0, rr, :, cols], cur[0, rr, 0:BLK, cols]], axis=0)
        block(rr, 0, window, lambda p: bias_ref[first_tile, p])
        return carry

    lax.fori_loop(0, res_per_step, first_block, 0, unroll=min(ATTN_UNROLL, res_per_step))

    rest = blk_per_step - 1
    if rest:
        def later_block(i, carry):
            rr, j = i // rest, 1 + i % rest
            k0 = pl.multiple_of((j - 1) * BLK, BLK)
            window = lambda prev, cur, cols: cur[0, rr, pl.ds(k0, 2 * BLK), cols]
            block(rr, j, window, lambda p: bias_ref[0, p])
            return carry

        lax.fori_loop(0, res_per_step * rest, later_block, 0, unroll=ATTN_UNROLL)

    if dilation > 1:
        @pl.when(rg == dilation // res_per_step - 1)
        def _():
            for p in range(N_PAIRS):
                o_ref[0, :, p * LANES:(p + 1) * LANES] = scratch[0][p].astype(BF16)


ATTN_STEP_SHAPE = {1: (1, 8), 4: (4, 2), 16: (8, 1)}


def _attention_group(q, k, v, group):
    _, d = ATTN_PAIRS[group]
    B, _, L, _ = q.shape
    S = L * d
    rb, qb = ATTN_STEP_SHAPE[d]
    rows = qb * BLK
    bias = _attn_bias(group, d)
    cur_spec = pl.BlockSpec((1, rb, rows, D_B), lambda b, n, rg: (b, rg, n, 0))
    prev_spec = pl.BlockSpec((1, rb, BLK, D_B), lambda b, n, rg: (b, rg, jnp.maximum(n * qb - 1, 0), 0))
    kern = functools.partial(_attn_kernel, dilation=d, res_per_step=rb, blk_per_step=qb)
    return pl.pallas_call(
        kern,
        grid=(B, L // rows, d // rb),
        in_specs=[cur_spec, prev_spec, cur_spec, prev_spec, cur_spec, _const_spec(bias.shape)],
        out_specs=[
            pl.BlockSpec((1, rows * d, D_B), lambda b, n, rg: (b, n, 0)),
            pl.BlockSpec((1, rows * d, LANES), lambda b, n, rg: (b, n, 0)),
        ],
        out_shape=[
            jax.ShapeDtypeStruct((B, S, D_B), BF16),
            jax.ShapeDtypeStruct((B, S, LANES), F32),
        ],
        scratch_shapes=[pltpu.VMEM((N_PAIRS, rows * d, LANES), F32)] if d > 1 else [],
        compiler_params=pltpu.CompilerParams(
            dimension_semantics=("arbitrary", "arbitrary", "arbitrary"),
            vmem_limit_bytes=VMEM_LIMIT_BYTES),
        name=f"attn_g{group}",
    )(q, k, k, v, v, bias)


def _head_expand_matrix():
    row = np.arange(LANES)[:, None]
    col = np.arange(D_B)[None, :]
    head = col // HEAD_DIM
    src = (head // 2) + (head % 2) * HEAD_DIM
    return jnp.asarray((row == src).astype(np.float32), dtype=BF16)


def _merge_kernel(o0_ref, o1_ref, o2_ref, s0_ref, s1_ref, s2_ref, gate_ref, h_ref,
                  e_ref, w_out_ref, g_ref, b_ref, *refs, plan, tile):
    w_refs, (out_ref, *proj_refs) = (refs[:1], refs[1:]) if plan else ((), refs)
    st = [s0_ref[0], s1_ref[0], s2_ref[0]]
    mx = jnp.maximum(jnp.maximum(st[0], st[1]), st[2])
    ex = [jnp.exp2(s - mx) for s in st]
    inv = 1.0 / (ex[0] + ex[1] + ex[2])
    e = e_ref[...]
    w0 = _dot((ex[0] * inv).astype(BF16), e)
    w1 = _dot((ex[1] * inv).astype(BF16), e)
    w2 = 1.0 - (w0 + w1)
    merged = (w0.astype(BF16) * o0_ref[0] + w1.astype(BF16) * o1_ref[0]
              + w2.astype(BF16) * o2_ref[0])
    act = merged * gate_ref[0]
    y = _dot(act, w_out_ref[...])
    h_new = _deepnorm_ln(h_ref[0], y, g_ref[...], b_ref[...])
    out_ref[0] = h_new
    if plan:
        _project_rows(h_new, w_refs, proj_refs[:-1], proj_refs[-1], plan, tile)


def _merge_layer(outs, stats, gate, h, w_out, ln_g, ln_b, w_next=None, plan=(), *, tile=None):
    B, S, D = h.shape
    tile = tile or (512 if plan else 1024)
    e = _head_expand_matrix()
    row = lambda width: pl.BlockSpec((1, tile, width), lambda b, s: (b, s, 0))
    proj_shape = [jax.ShapeDtypeStruct((B, d, S // d, D_B), BF16) for _, _, d, _ in plan]
    proj_specs = [pl.BlockSpec((1, d, tile // d, D_B), lambda b, s: (b, 0, s, 0)) for _, _, d, _ in plan]
    weights = [w_next] if plan else []
    return pl.pallas_call(
        functools.partial(_merge_kernel, plan=tuple(plan), tile=tile),
        grid=(B, S // tile),
        in_specs=[row(D_B)] * 3 + [row(LANES)] * 3 + [row(D_B), row(D)] + [
            _const_spec(e.shape), _const_spec(w_out.shape),
            _const_spec(ln_g.shape), _const_spec(ln_b.shape)] + [_const_spec(w.shape) for w in weights],
        out_specs=[row(D)] + proj_specs,
        out_shape=[jax.ShapeDtypeStruct((B, S, D), F32)] + proj_shape,
        scratch_shapes=[pltpu.VMEM((D // LANES, tile, LANES), F32)] if plan else [],
        compiler_params=pltpu.CompilerParams(
            dimension_semantics=("arbitrary", "arbitrary"),
            vmem_limit_bytes=VMEM_LIMIT_BYTES),
        name="merge_layer",
    )(*outs, *stats, gate, h, e, w_out, ln_g, ln_b, *weights)


def kernel(x, w_in_a, w_grp_a, scale_a, w_out_a, w_kv, w_in_b, w_out_b, ln_g, ln_b):
    n_q = N_GROUPS * D_B
    B, S, _ = x.shape
    dils = [d for _, d in ATTN_PAIRS]
    q_scale = jnp.concatenate([jnp.full((n_q,), HEAD_DIM ** -0.5 * LOG2_E, F32), jnp.ones((D_B,), F32)])
    def kv_plan(wi):
        return ([(wi, grp * D_B, d, False) for grp, d in enumerate(dils)]
                + [(wi, n_q + grp * D_B, d, False) for grp, d in enumerate(dils)])

    def q_plan(wi):
        return [(wi, grp * D_B, d, False) for grp, d in enumerate(dils)] + [(wi, n_q, 1, True)]

    w_q = [(w_in_b[j] * q_scale).astype(BF16) for j in range(DEPTH - N_A_LAYERS)]
    h = x
    ks = vs = qs = gate = None
    for l in range(DEPTH):
        g = ln_g[l][None, :]
        b = ln_b[l][None, :]
        if l < N_A_LAYERS:
            h = _pool_layer(h, w_in_a[l].astype(BF16), w_grp_a[l].astype(BF16),
                            scale_a[l][None, :], w_out_a[l].astype(BF16), g, b)
            if l == N_A_LAYERS - 1:
                outs = _project(h, [w_kv.astype(BF16), w_q[0]], kv_plan(0) + q_plan(1))
                ks, vs = outs[:N_GROUPS], outs[N_GROUPS:2 * N_GROUPS]
                *qs, gate = outs[2 * N_GROUPS:]
        else:
            j = l - N_A_LAYERS
            outs, stats = zip(*[_attention_group(qs[grp], ks[grp], vs[grp], grp)
                                for grp in range(N_GROUPS)])
            last = l == DEPTH - 1
            h, *nxt = _merge_layer(outs, stats, gate.reshape(B, S, D_B), h, w_out_b[j].astype(BF16), g, b,
                                   *(() if last else (w_q[j + 1], q_plan(0))))
            if not last:
                *qs, gate = nxt
    return h
```

```python
import functools

import jax
import jax.numpy as jnp
import numpy as np
from jax import lax
from jax.experimental import pallas as pl
from jax.experimental.pallas import tpu as pltpu

D_MODEL = 1024
DEPTH = 4
N_A_LAYERS = DEPTH // 2
D_A = 2 * D_MODEL
POOL_WINDOWS = (2, 4, 8, 16)
N_POOL_GROUPS = len(POOL_WINDOWS)
POOL_GW = D_A // N_POOL_GROUPS
ATTN_PAIRS = ((128, 1), (512, 4), (2048, 16))
N_GROUPS = len(ATTN_PAIRS)
N_HEADS = 16
HEAD_DIM = D_MODEL // N_HEADS
D_B = N_HEADS * HEAD_DIM
BLK = 128
DN_ALPHA = (2 * DEPTH) ** 0.25
LN_EPS = 1e-5
NEG_INF = -1e30
LOG2_E = 1.4426950408889634

LANES = 128
N_PAIRS = N_HEADS // 2
SUBLANES = 8
POOL_HALO = SUBLANES * N_POOL_GROUPS
VMEM_LIMIT_BYTES = 56 * 1024 * 1024
ATTN_UNROLL = 8

F32 = jnp.float32
BF16 = jnp.bfloat16


def _dot(a, b):
    return jnp.dot(a, b, preferred_element_type=F32)


def _silu(x):
    hx = 0.5 * x
    return hx + hx * jnp.tanh(hx)


def _deepnorm_ln(h, y, g, b):
    z = DN_ALPHA * h + y
    mu = jnp.mean(z, axis=-1, keepdims=True)
    zc = z - mu
    var = jnp.mean(zc * zc, axis=-1, keepdims=True)
    return zc * lax.rsqrt(var + LN_EPS) * g + b


def _const_spec(shape):
    nd = len(shape)
    return pl.BlockSpec(shape, lambda *_: (0,) * nd, pipeline_mode=pl.Buffered(1))


def _fold_kernel(w_u_ref, w_grp_ref, scale_ref, o_ref):
    o_ref[...] = (_dot(w_u_ref[...].astype(BF16), w_grp_ref[0].astype(BF16)) * scale_ref[...]).astype(BF16)


def _fold_pool_weights(w_in, w_grp, scale):
    D = w_in.shape[0]
    return pl.pallas_call(
        _fold_kernel,
        grid=(N_POOL_GROUPS,),
        in_specs=[
            pl.BlockSpec((D, POOL_GW), lambda g: (0, g)),
            pl.BlockSpec((1, POOL_GW, POOL_GW), lambda g: (g, 0, 0)),
            pl.BlockSpec((1, POOL_GW), lambda g: (0, g)),
        ],
        out_specs=pl.BlockSpec((D, POOL_GW), lambda g: (0, g)),
        out_shape=jax.ShapeDtypeStruct((D, D_A), BF16),
        compiler_params=pltpu.CompilerParams(dimension_semantics=("arbitrary",)),
        name="fold_pool_weights",
    )(w_in, w_grp, scale)


def _pool_layer_kernel(h_ref, w_mix_ref, w_gate_ref, w_out_ref, g_ref, b_ref,
                       o_ref, hext_ref, *sum_refs, tile):
    s = pl.program_id(1)
    halo, end = POOL_HALO, POOL_HALO + tile

    @pl.when(s == 0)
    def _():
        hext_ref[0:halo, :] = jnp.zeros((halo, D_MODEL), F32)

    h = h_ref[0]
    hb = h.astype(BF16)
    gates = [_dot(hb, w_gate_ref[:, g * POOL_GW:(g + 1) * POOL_GW]) for g in range(N_POOL_GROUPS)]
    hext_ref[halo:end, :] = h
    src, w, lo = hext_ref, 1, 0
    window_sums = []
    for dst in sum_refs:
        lo += SUBLANES
        dst[lo:end, :] = src[lo:end, :] + src[lo - w:end - w, :]
        window_sums.append(dst[halo:end, :])
        src, w = dst, 2 * w
    window_sums.append(src[halo:end, :] + src[halo - w:end - w, :])

    pos = s * tile + lax.broadcasted_iota(jnp.int32, (tile, 1), 0)
    y = jnp.zeros((tile, D_MODEL), F32)
    for g, (w, wsum) in enumerate(zip(POOL_WINDOWS, window_sums)):
        c0 = g * POOL_GW
        inv_cnt = 1.0 / jnp.minimum(pos + 1, w).astype(F32)
        lhs = (wsum * inv_cnt - h).astype(BF16)
        mixed = _dot(lhs, w_mix_ref[:, c0:c0 + POOL_GW])
        act = mixed * _silu(gates[g])
        y = y + _dot(act.astype(BF16), w_out_ref[c0:c0 + POOL_GW, :])
    hext_ref[0:halo, :] = hext_ref[tile:end, :]
    o_ref[0] = _deepnorm_ln(h, y, g_ref[...], b_ref[...])


def _pool_layer(h, w_in, w_grp, scale, w_out, ln_g, ln_b, *, tile=512):
    B, S, D = h.shape
    w_mix = _fold_pool_weights(w_in, w_grp, scale)
    w_gate = w_in[:, D_A:].astype(BF16)
    kern = functools.partial(_pool_layer_kernel, tile=tile)
    return pl.pallas_call(
        kern,
        grid=(B, S // tile),
        in_specs=[
            pl.BlockSpec((1, tile, D), lambda b, s: (b, s, 0)),
            _const_spec(w_mix.shape),
            _const_spec(w_gate.shape),
            _const_spec(w_out.shape),
            _const_spec(ln_g.shape),
            _const_spec(ln_b.shape),
        ],
        out_specs=pl.BlockSpec((1, tile, D), lambda b, s: (b, s, 0)),
        out_shape=jax.ShapeDtypeStruct((B, S, D), F32),
        scratch_shapes=[pltpu.VMEM((POOL_HALO + tile, D), F32)] * N_POOL_GROUPS,
        compiler_params=pltpu.CompilerParams(
            dimension_semantics=("arbitrary", "arbitrary"),
            vmem_limit_bytes=VMEM_LIMIT_BYTES),
        name="pool_layer",
    )(h, w_mix, w_gate, w_out, ln_g, ln_b)


def _proj_kernel(x_ref, *refs, plan, tile, n_weights):
    w_refs = refs[:n_weights]
    out_refs, slab_ref = refs[n_weights:n_weights + len(plan)], refs[n_weights + len(plan)]
    _project_rows(x_ref[0], w_refs, out_refs, slab_ref, plan, tile)


def _project_rows(x, w_refs, out_refs, slab_ref, plan, tile):
    lhs = {1: x.astype(BF16)}
    dils = sorted({d for _, _, d, _ in plan if d > 1})
    if dils:
        for j in range(D_MODEL // LANES):
            slab_ref[j] = x[:, j * LANES:(j + 1) * LANES]
    for d in dils:
        planes = [
            jnp.concatenate([slab_ref[j, pl.ds(r, tile // d, stride=d), :]
                             for j in range(D_MODEL // LANES)], axis=1)
            for r in range(d)]
        lhs[d] = jnp.concatenate(planes, axis=0).astype(BF16)
    for (wi, c0, d, gated), o_ref in zip(plan, out_refs):
        res = _dot(lhs[d], w_refs[wi][:, c0:c0 + D_B])
        res = (_silu(res) if gated else res).astype(o_ref.dtype)
        rows = tile // d
        for r in range(d):
            o_ref[0, r] = res[r * rows:(r + 1) * rows]


def _project(x, weights, plan, *, tile=512):
    B, S, D = x.shape
    out_shape = [jax.ShapeDtypeStruct((B, d, S // d, D_B), BF16) for _, _, d, _ in plan]
    out_specs = [pl.BlockSpec((1, d, tile // d, D_B), lambda b, s: (b, 0, s, 0)) for _, _, d, _ in plan]
    kern = functools.partial(_proj_kernel, plan=tuple(plan), tile=tile, n_weights=len(weights))
    return pl.pallas_call(
        kern,
        grid=(B, S // tile),
        in_specs=[pl.BlockSpec((1, tile, D), lambda b, s: (b, s, 0))]
                 + [_const_spec(w.shape) for w in weights],
        out_specs=out_specs,
        out_shape=out_shape,
        scratch_shapes=[pltpu.VMEM((D // LANES, tile, LANES), F32)],
        compiler_params=pltpu.CompilerParams(
            dimension_semantics=("arbitrary", "arbitrary"),
            vmem_limit_bytes=VMEM_LIMIT_BYTES),
        name="project",
    )(x, *weights)


def _attn_bias(group, dilation):
    n = N_GROUPS * N_HEADS
    i = np.arange(1, n + 1, dtype=np.float32)
    slopes = np.exp2(np.float32(-8.0) * i / np.float32(n)).reshape(N_GROUPS, N_HEADS)[group]
    a = np.arange(BLK)[:, None]
    c = np.arange(2 * BLK)[None, :]
    rel = BLK + a - c
    win_sub = ATTN_PAIRS[group][0] // dilation
    valid = (rel >= 0) & (rel <= win_sub)
    bias = (-slopes[:, None, None] * (dilation * rel).astype(np.float32)[None]
            * np.float32(LOG2_E)).astype(np.float32)
    later = np.where(valid[None], bias, np.float32(NEG_INF))
    first = np.where((valid & (c >= BLK))[None], bias, np.float32(NEG_INF))
    table = np.stack([later, first], axis=0).reshape(2, N_PAIRS, 2 * BLK, 2 * BLK)
    return jnp.asarray(table, dtype=F32)


def _attn_kernel(q_ref, kp_ref, kc_ref, vp_ref, vc_ref, bias_ref, o_ref, st_ref, *scratch,
                 dilation, res_per_step, blk_per_step):
    n = pl.program_id(1)
    rg = pl.program_id(2)
    lane = lax.broadcasted_iota(jnp.int32, (1, LANES), 1)
    lo = lane < HEAD_DIM
    ones = jnp.ones((2 * BLK, LANES), BF16)

    def block(rr, j, key_window, bias_tile):
        row0 = j * BLK if isinstance(j, int) else pl.multiple_of(j * BLK, BLK)
        out0 = row0 * dilation + rg * res_per_step + rr
        stats = jnp.zeros((BLK, LANES), F32)
        for p in range(N_PAIRS):
            cols = slice(p * LANES, (p + 1) * LANES)
            qp = q_ref[0, rr, pl.ds(row0, BLK), cols]
            kp = key_window(kp_ref, kc_ref, cols)
            vp = key_window(vp_ref, vc_ref, cols)
            zero = jnp.zeros_like(qp)
            q2 = jnp.concatenate([jnp.where(lo, qp, zero), jnp.where(lo, zero, qp)], axis=0)
            sc = lax.dot_general(q2, kp, (((1,), (1,)), ((), ())), preferred_element_type=F32)
            sc = sc + bias_tile(p)
            m = jnp.max(sc, axis=-1, keepdims=True)
            pe = jnp.exp2((sc - m).astype(BF16))
            acc = _dot(pe, jnp.concatenate([vp, ones], axis=1))
            pick = lambda x: jnp.where(lo, x[:BLK], x[BLK:])
            denom = pick(acc[:, LANES:])
            m_pair = pick(m)
            o = pick(acc[:, :LANES]) / denom
            if dilation == 1:
                o_ref[0, pl.ds(row0, BLK), cols] = o.astype(BF16)
            else:
                scratch[0][p, pl.ds(out0, BLK, stride=dilation), :] = o
            lse = m_pair + jnp.log2(denom)
            stats = jnp.where((lane & (HEAD_DIM - 1)) == p, lse, stats)
        if dilation == 1:
            st_ref[0, pl.ds(row0, BLK), :] = stats
        else:
            st_ref[0, pl.ds(out0, BLK, stride=dilation), :] = stats

    first_tile = (n == 0).astype(jnp.int32)

    def first_block(rr, carry):
        window = lambda prev, cur, cols: jnp.concatenate(
            [prev[0, rr, :, cols], cur[0, rr, 0:BLK, cols]], axis=0)
        block(rr, 0, window, lambda p: bias_ref[first_tile, p])
        return carry

    lax.fori_loop(0, res_per_step, first_block, 0, unroll=min(ATTN_UNROLL, res_per_step))

    rest = blk_per_step - 1
    if rest:
        def later_block(i, carry):
            rr, j = i // rest, 1 + i % rest
            k0 = pl.multiple_of((j - 1) * BLK, BLK)
            window = lambda prev, cur, cols: cur[0, rr, pl.ds(k0, 2 * BLK), cols]
            block(rr, j, window, lambda p: bias_ref[0, p])
            return carry

        lax.fori_loop(0, res_per_step * rest, later_block, 0, unroll=ATTN_UNROLL)

    if dilation > 1:
        @pl.when(rg == dilation // res_per_step - 1)
        def _():
            for p in range(N_PAIRS):
                o_ref[0, :, p * LANES:(p + 1) * LANES] = scratch[0][p].astype(BF16)


ATTN_STEP_SHAPE = {1: (1, 8), 4: (4, 2), 16: (8, 1)}


def _attention_group(q, k, v, group):
    _, d = ATTN_PAIRS[group]
    B, _, L, _ = q.shape
    S = L * d
    rb, qb = ATTN_STEP_SHAPE[d]
    rows = qb * BLK
    bias = _attn_bias(group, d)
    cur_spec = pl.BlockSpec((1, rb, rows, D_B), lambda b, n, rg: (b, rg, n, 0))
    prev_spec = pl.BlockSpec((1, rb, BLK, D_B), lambda b, n, rg: (b, rg, jnp.maximum(n * qb - 1, 0), 0))
    kern = functools.partial(_attn_kernel, dilation=d, res_per_step=rb, blk_per_step=qb)
    return pl.pallas_call(
        kern,
        grid=(B, L // rows, d // rb),
        in_specs=[cur_spec, prev_spec, cur_spec, prev_spec, cur_spec, _const_spec(bias.shape)],
        out_specs=[
            pl.BlockSpec((1, rows * d, D_B), lambda b, n, rg: (b, n, 0)),
            pl.BlockSpec((1, rows * d, LANES), lambda b, n, rg: (b, n, 0)),
        ],
        out_shape=[
            jax.ShapeDtypeStruct((B, S, D_B), BF16),
            jax.ShapeDtypeStruct((B, S, LANES), F32),
        ],
        scratch_shapes=[pltpu.VMEM((N_PAIRS, rows * d, LANES), F32)] if d > 1 else [],
        compiler_params=pltpu.CompilerParams(
            dimension_semantics=("arbitrary", "arbitrary", "arbitrary"),
            vmem_limit_bytes=VMEM_LIMIT_BYTES),
        name=f"attn_g{group}",
    )(q, k, k, v, v, bias)


def _head_expand_matrix():
    row = np.arange(LANES)[:, None]
    col = np.arange(D_B)[None, :]
    head = col // HEAD_DIM
    src = (head // 2) + (head % 2) * HEAD_DIM
    return jnp.asarray((row == src).astype(np.float32), dtype=BF16)


def _merge_kernel(o0_ref, o1_ref, o2_ref, s0_ref, s1_ref, s2_ref, gate_ref, h_ref,
                  e_ref, w_out_ref, g_ref, b_ref, *refs, plan, tile):
    w_refs, (out_ref, *proj_refs) = (refs[:1], refs[1:]) if plan else ((), refs)
    st = [s0_ref[0], s1_ref[0], s2_ref[0]]
    mx = jnp.maximum(jnp.maximum(st[0], st[1]), st[2])
    ex = [jnp.exp2(s - mx) for s in st]
    inv = 1.0 / (ex[0] + ex[1] + ex[2])
    e = e_ref[...]
    w0 = _dot((ex[0] * inv).astype(BF16), e)
    w1 = _dot((ex[1] * inv).astype(BF16), e)
    w2 = 1.0 - (w0 + w1)
    merged = (w0.astype(BF16) * o0_ref[0] + w1.astype(BF16) * o1_ref[0]
              + w2.astype(BF16) * o2_ref[0])
    act = merged * gate_ref[0]
    y = _dot(act, w_out_ref[...])
    h_new = _deepnorm_ln(h_ref[0], y, g_ref[...], b_ref[...])
    out_ref[0] = h_new
    if plan:
        _project_rows(h_new, w_refs, proj_refs[:-1], proj_refs[-1], plan, tile)


def _merge_layer(outs, stats, gate, h, w_out, ln_g, ln_b, w_next=None, plan=(), *, tile=None):
    B, S, D = h.shape
    tile = tile or (512 if plan else 1024)
    e = _head_expand_matrix()
    row = lambda width: pl.BlockSpec((1, tile, width), lambda b, s: (b, s, 0))
    proj_shape = [jax.ShapeDtypeStruct((B, d, S // d, D_B), BF16) for _, _, d, _ in plan]
    proj_specs = [pl.BlockSpec((1, d, tile // d, D_B), lambda b, s: (b, 0, s, 0)) for _, _, d, _ in plan]
    weights = [w_next] if plan else []
    return pl.pallas_call(
        functools.partial(_merge_kernel, plan=tuple(plan), tile=tile),
        grid=(B, S // tile),
        in_specs=[row(D_B)] * 3 + [row(LANES)] * 3 + [row(D_B), row(D)] + [
            _const_spec(e.shape), _const_spec(w_out.shape),
            _const_spec(ln_g.shape), _const_spec(ln_b.shape)] + [_const_spec(w.shape) for w in weights],
        out_specs=[row(D)] + proj_specs,
        out_shape=[jax.ShapeDtypeStruct((B, S, D), F32)] + proj_shape,
        scratch_shapes=[pltpu.VMEM((D // LANES, tile, LANES), F32)] if plan else [],
        compiler_params=pltpu.CompilerParams(
            dimension_semantics=("arbitrary", "arbitrary"),
            vmem_limit_bytes=VMEM_LIMIT_BYTES),
        name="merge_layer",
    )(*outs, *stats, gate, h, e, w_out, ln_g, ln_b, *weights)


def kernel(x, w_in_a, w_grp_a, scale_a, w_out_a, w_kv, w_in_b, w_out_b, ln_g, ln_b):
    n_q = N_GROUPS * D_B
    B, S, _ = x.shape
    dils = [d for _, d in ATTN_PAIRS]
    q_scale = jnp.concatenate([jnp.full((n_q,), HEAD_DIM ** -0.5 * LOG2_E, F32), jnp.ones((D_B,), F32)])
    def kv_plan(wi):
        return ([(wi, grp * D_B, d, False) for grp, d in enumerate(dils)]
                + [(wi, n_q + grp * D_B, d, False) for grp, d in enumerate(dils)])

    def q_plan(wi):
        return [(wi, grp * D_B, d, False) for grp, d in enumerate(dils)] + [(wi, n_q, 1, True)]

    w_q = [(w_in_b[j] * q_scale).astype(BF16) for j in range(DEPTH - N_A_LAYERS)]
    h = x
    ks = vs = qs = gate = None
    for l in range(DEPTH):
        g = ln_g[l][None, :]
        b = ln_b[l][None, :]
        if l < N_A_LAYERS:
            h = _pool_layer(h, w_in_a[l], w_grp_a[l], scale_a[l][None, :], w_out_a[l].astype(BF16), g, b)
            if l == N_A_LAYERS - 1:
                outs = _project(h, [w_kv.astype(BF16), w_q[0]], kv_plan(0) + q_plan(1))
                ks, vs = outs[:N_GROUPS], outs[N_GROUPS:2 * N_GROUPS]
                *qs, gate = outs[2 * N_GROUPS:]
        else:
            j = l - N_A_LAYERS
            outs, stats = zip(*[_attention_group(qs[grp], ks[grp], vs[grp], grp)
                                for grp in range(N_GROUPS)])
            last = l == DEPTH - 1
            h, *nxt = _merge_layer(outs, stats, gate.reshape(B, S, D_B), h, w_out_b[j].astype(BF16), g, b,
                                   *(() if last else (w_q[j + 1], q_plan(0))))
            if not last:
                *qs, gate = nxt
    return h
```

```python
import functools

import jax
import jax.numpy as jnp
import numpy as np
from jax import lax
from jax.experimental import pallas as pl
from jax.experimental.pallas import tpu as pltpu

D_MODEL = 1024
DEPTH = 4
N_A_LAYERS = DEPTH // 2
D_A = 2 * D_MODEL
POOL_WINDOWS = (2, 4, 8, 16)
N_POOL_GROUPS = len(POOL_WINDOWS)
POOL_GW = D_A // N_POOL_GROUPS
ATTN_PAIRS = ((128, 1), (512, 4), (2048, 16))
N_GROUPS = len(ATTN_PAIRS)
N_HEADS = 16
HEAD_DIM = D_MODEL // N_HEADS
D_B = N_HEADS * HEAD_DIM
BLK = 128
DN_ALPHA = (2 * DEPTH) ** 0.25
LN_EPS = 1e-5
NEG_INF = -1e30
LOG2_E = 1.4426950408889634

LANES = 128
N_PAIRS = N_HEADS // 2
SUBLANES = 8
POOL_HALO = SUBLANES * N_POOL_GROUPS
VMEM_LIMIT_BYTES = 56 * 1024 * 1024
ATTN_UNROLL = 8

F32 = jnp.float32
BF16 = jnp.bfloat16


def _dot(a, b):
    return jnp.dot(a, b, preferred_element_type=F32)


def _silu(x):
    hx = 0.5 * x
    return hx + hx * jnp.tanh(hx)


def _deepnorm_ln(h, y, g, b):
    z = DN_ALPHA * h + y
    mu = jnp.mean(z, axis=-1, keepdims=True)
    zc = z - mu
    var = jnp.mean(zc * zc, axis=-1, keepdims=True)
    return zc * lax.rsqrt(var + LN_EPS) * g + b


def _const_spec(shape):
    nd = len(shape)
    return pl.BlockSpec(shape, lambda *_: (0,) * nd, pipeline_mode=pl.Buffered(1))


def _fold_kernel(w_u_ref, w_grp_ref, scale_ref, o_ref):
    o_ref[...] = (_dot(w_u_ref[...], w_grp_ref[0]) * scale_ref[...]).astype(BF16)


def _fold_pool_weights(w_in, w_grp, scale):
    D = w_in.shape[0]
    return pl.pallas_call(
        _fold_kernel,
        grid=(N_POOL_GROUPS,),
        in_specs=[
            pl.BlockSpec((D, POOL_GW), lambda g: (0, g)),
            pl.BlockSpec((1, POOL_GW, POOL_GW), lambda g: (g, 0, 0)),
            pl.BlockSpec((1, POOL_GW), lambda g: (0, g)),
        ],
        out_specs=pl.BlockSpec((D, POOL_GW), lambda g: (0, g)),
        out_shape=jax.ShapeDtypeStruct((D, D_A), BF16),
        compiler_params=pltpu.CompilerParams(dimension_semantics=("arbitrary",)),
        name="fold_pool_weights",
    )(w_in, w_grp, scale)


def _pool_layer_kernel(h_ref, w_mix_ref, w_gate_ref, w_out_ref, g_ref, b_ref,
                       o_ref, hext_ref, *sum_refs, tile):
    s = pl.program_id(1)
    halo, end = POOL_HALO, POOL_HALO + tile

    @pl.when(s == 0)
    def _():
        hext_ref[0:halo, :] = jnp.zeros((halo, D_MODEL), F32)

    h = h_ref[0]
    hb = h.astype(BF16)
    gates = [_dot(hb, w_gate_ref[:, g * POOL_GW:(g + 1) * POOL_GW]) for g in range(N_POOL_GROUPS)]
    hext_ref[halo:end, :] = h
    src, w, lo = hext_ref, 1, 0
    window_sums = []
    for dst in sum_refs:
        lo += SUBLANES
        dst[lo:end, :] = src[lo:end, :] + src[lo - w:end - w, :]
        window_sums.append(dst[halo:end, :])
        src, w = dst, 2 * w
    window_sums.append(src[halo:end, :] + src[halo - w:end - w, :])

    pos = s * tile + lax.broadcasted_iota(jnp.int32, (tile, 1), 0)
    y = jnp.zeros((tile, D_MODEL), F32)
    for g, (w, wsum) in enumerate(zip(POOL_WINDOWS, window_sums)):
        c0 = g * POOL_GW
        inv_cnt = 1.0 / jnp.minimum(pos + 1, w).astype(F32)
        lhs = (wsum * inv_cnt - h).astype(BF16)
        mixed = _dot(lhs, w_mix_ref[:, c0:c0 + POOL_GW])
        act = mixed * _silu(gates[g])
        y = y + _dot(act.astype(BF16), w_out_ref[c0:c0 + POOL_GW, :])
    hext_ref[0:halo, :] = hext_ref[tile:end, :]
    o_ref[0] = _deepnorm_ln(h, y, g_ref[...], b_ref[...])


def _pool_layer(h, w_in, w_grp, scale, w_out, ln_g, ln_b, *, tile=512):
    B, S, D = h.shape
    w_mix = _fold_pool_weights(w_in, w_grp, scale)
    kern = functools.partial(_pool_layer_kernel, tile=tile)
    return pl.pallas_call(
        kern,
        grid=(B, S // tile),
        in_specs=[
            pl.BlockSpec((1, tile, D), lambda b, s: (b, s, 0)),
            _const_spec(w_mix.shape),
            pl.BlockSpec((D, D_A), lambda b, s: (0, 1), pipeline_mode=pl.Buffered(1)),
            _const_spec(w_out.shape),
            _const_spec(ln_g.shape),
            _const_spec(ln_b.shape),
        ],
        out_specs=pl.BlockSpec((1, tile, D), lambda b, s: (b, s, 0)),
        out_shape=jax.ShapeDtypeStruct((B, S, D), F32),
        scratch_shapes=[pltpu.VMEM((POOL_HALO + tile, D), F32)] * N_POOL_GROUPS,
        compiler_params=pltpu.CompilerParams(
            dimension_semantics=("arbitrary", "arbitrary"),
            vmem_limit_bytes=VMEM_LIMIT_BYTES),
        name="pool_layer",
    )(h, w_mix, w_in, w_out, ln_g, ln_b)


def _proj_kernel(x_ref, *refs, plan, tile, n_weights):
    w_refs = refs[:n_weights]
    out_refs, slab_ref = refs[n_weights:n_weights + len(plan)], refs[n_weights + len(plan)]
    _project_rows(x_ref[0], w_refs, out_refs, slab_ref, plan, tile)


def _project_rows(x, w_refs, out_refs, slab_ref, plan, tile):
    lhs = {1: x.astype(BF16)}
    dils = sorted({d for _, _, d, _ in plan if d > 1})
    if dils:
        for j in range(D_MODEL // LANES):
            slab_ref[j] = x[:, j * LANES:(j + 1) * LANES]
    for d in dils:
        planes = [
            jnp.concatenate([slab_ref[j, pl.ds(r, tile // d, stride=d), :]
                             for j in range(D_MODEL // LANES)], axis=1)
            for r in range(d)]
        lhs[d] = jnp.concatenate(planes, axis=0).astype(BF16)
    for (wi, c0, d, gated), o_ref in zip(plan, out_refs):
        res = _dot(lhs[d], w_refs[wi][:, c0:c0 + D_B])
        res = (_silu(res) if gated else res).astype(o_ref.dtype)
        rows = tile // d
        for r in range(d):
            o_ref[0, r] = res[r * rows:(r + 1) * rows]


def _project(x, weights, plan, *, tile=512):
    B, S, D = x.shape
    out_shape = [jax.ShapeDtypeStruct((B, d, S // d, D_B), BF16) for _, _, d, _ in plan]
    out_specs = [pl.BlockSpec((1, d, tile // d, D_B), lambda b, s: (b, 0, s, 0)) for _, _, d, _ in plan]
    kern = functools.partial(_proj_kernel, plan=tuple(plan), tile=tile, n_weights=len(weights))
    return pl.pallas_call(
        kern,
        grid=(B, S // tile),
        in_specs=[pl.BlockSpec((1, tile, D), lambda b, s: (b, s, 0))]
                 + [_const_spec(w.shape) for w in weights],
        out_specs=out_specs,
        out_shape=out_shape,
        scratch_shapes=[pltpu.VMEM((D // LANES, tile, LANES), F32)],
        compiler_params=pltpu.CompilerParams(
            dimension_semantics=("arbitrary", "arbitrary"),
            vmem_limit_bytes=VMEM_LIMIT_BYTES),
        name="project",
    )(x, *weights)


def _attn_bias(group, dilation):
    n = N_GROUPS * N_HEADS
    i = np.arange(1, n + 1, dtype=np.float32)
    slopes = np.exp2(np.float32(-8.0) * i / np.float32(n)).reshape(N_GROUPS, N_HEADS)[group]
    a = np.arange(BLK)[:, None]
    c = np.arange(2 * BLK)[None, :]
    rel = BLK + a - c
    win_sub = ATTN_PAIRS[group][0] // dilation
    valid = (rel >= 0) & (rel <= win_sub)
    bias = (-slopes[:, None, None] * (dilation * rel).astype(np.float32)[None]
            * np.float32(LOG2_E)).astype(np.float32)
    later = np.where(valid[None], bias, np.float32(NEG_INF))
    first = np.where((valid & (c >= BLK))[None], bias, np.float32(NEG_INF))
    table = np.stack([later, first], axis=0).reshape(2, N_PAIRS, 2 * BLK, 2 * BLK)
    return jnp.asarray(table, dtype=F32)


def _attn_kernel(q_ref, kp_ref, kc_ref, vp_ref, vc_ref, bias_ref, o_ref, st_ref, *scratch,
                 dilation, res_per_step, blk_per_step):
    n = pl.program_id(1)
    rg = pl.program_id(2)
    lane = lax.broadcasted_iota(jnp.int32, (1, LANES), 1)
    lo = lane < HEAD_DIM
    ones = jnp.ones((2 * BLK, LANES), BF16)

    def block(rr, j, key_window, bias_tile):
        row0 = j * BLK if isinstance(j, int) else pl.multiple_of(j * BLK, BLK)
        out0 = row0 * dilation + rg * res_per_step + rr
        stats = jnp.zeros((BLK, LANES), F32)
        for p in range(N_PAIRS):
            cols = slice(p * LANES, (p + 1) * LANES)
            qp = q_ref[0, rr, pl.ds(row0, BLK), cols]
            kp = key_window(kp_ref, kc_ref, cols)
            vp = key_window(vp_ref, vc_ref, cols)
            zero = jnp.zeros_like(qp)
            q2 = jnp.concatenate([jnp.where(lo, qp, zero), jnp.where(lo, zero, qp)], axis=0)
            sc = lax.dot_general(q2, kp, (((1,), (1,)), ((), ())), preferred_element_type=F32)
            sc = sc + bias_tile(p)
            m = jnp.max(sc, axis=-1, keepdims=True)
            pe = jnp.exp2((sc - m).astype(BF16))
            acc = _dot(pe, jnp.concatenate([vp, ones], axis=1))
            pick = lambda x: jnp.where(lo, x[:BLK], x[BLK:])
            denom = pick(acc[:, LANES:])
            m_pair = pick(m)
            o = pick(acc[:, :LANES]) / denom
            if dilation == 1:
                o_ref[0, pl.ds(row0, BLK), cols] = o.astype(BF16)
            else:
                scratch[0][p, pl.ds(out0, BLK, stride=dilation), :] = o
            lse = m_pair + jnp.log2(denom)
            stats = jnp.where((lane & (HEAD_DIM - 1)) == p, lse, stats)
        if dilation == 1:
            st_ref[0, pl.ds(row0, BLK), :] = stats
        else:
            st_ref[0, pl.ds(out0, BLK, stride=dilation), :] = stats

    first_tile = (n == 0).astype(jnp.int32)

    def first_block(rr, carry):
        window = lambda prev, cur, cols: jnp.concatenate(
            [prev[0, rr, :, cols], cur[0, rr, 0:BLK, cols]], axis=0)
        block(rr, 0, window, lambda p: bias_ref[first_tile, p])
        return carry

    lax.fori_loop(0, res_per_step, first_block, 0, unroll=min(ATTN_UNROLL, res_per_step))

    rest = blk_per_step - 1
    if rest:
        def later_block(i, carry):
            rr, j = i // rest, 1 + i % rest
            k0 = pl.multiple_of((j - 1) * BLK, BLK)
            window = lambda prev, cur, cols: cur[0, rr, pl.ds(k0, 2 * BLK), cols]
            block(rr, j, window, lambda p: bias_ref[0, p])
            return carry

        lax.fori_loop(0, res_per_step * rest, later_block, 0, unroll=ATTN_UNROLL)

    if dilation > 1:
        @pl.when(rg == dilation // res_per_step - 1)
        def _():
            for p in range(N_PAIRS):
                o_ref[0, :, p * LANES:(p + 1) * LANES] = scratch[0][p].astype(BF16)


ATTN_STEP_SHAPE = {1: (1, 16), 4: (4, 4), 16: (8, 1)}


def _attention_group(q, k, v, group):
    _, d = ATTN_PAIRS[group]
    B, _, L, _ = q.shape
    S = L * d
    rb, qb = ATTN_STEP_SHAPE[d]
    rows = qb * BLK
    bias = _attn_bias(group, d)
    cur_spec = pl.BlockSpec((1, rb, rows, D_B), lambda b, n, rg: (b, rg, n, 0))
    prev_spec = pl.BlockSpec((1, rb, BLK, D_B), lambda b, n, rg: (b, rg, jnp.maximum(n * qb - 1, 0), 0))
    kern = functools.partial(_attn_kernel, dilation=d, res_per_step=rb, blk_per_step=qb)
    return pl.pallas_call(
        kern,
        grid=(B, L // rows, d // rb),
        in_specs=[cur_spec, prev_spec, cur_spec, prev_spec, cur_spec, _const_spec(bias.shape)],
        out_specs=[
            pl.BlockSpec((1, rows * d, D_B), lambda b, n, rg: (b, n, 0)),
            pl.BlockSpec((1, rows * d, LANES), lambda b, n, rg: (b, n, 0)),
        ],
        out_shape=[
            jax.ShapeDtypeStruct((B, S, D_B), BF16),
            jax.ShapeDtypeStruct((B, S, LANES), F32),
        ],
        scratch_shapes=[pltpu.VMEM((N_PAIRS, rows * d, LANES), F32)] if d > 1 else [],
        compiler_params=pltpu.CompilerParams(
            dimension_semantics=("arbitrary", "arbitrary", "arbitrary"),
            vmem_limit_bytes=VMEM_LIMIT_BYTES),
        name=f"attn_g{group}",
    )(q, k, k, v, v, bias)


def _head_expand_matrix():
    row = np.arange(LANES)[:, None]
    col = np.arange(D_B)[None, :]
    head = col // HEAD_DIM
    src = (head // 2) + (head % 2) * HEAD_DIM
    return jnp.asarray((row == src).astype(np.float32), dtype=BF16)


def _merge_kernel(o0_ref, o1_ref, o2_ref, s0_ref, s1_ref, s2_ref, gate_ref, h_ref,
                  e_ref, w_out_ref, g_ref, b_ref, *refs, plan, tile):
    w_refs, (out_ref, *proj_refs) = (refs[:1], refs[1:]) if plan else ((), refs)
    st = [s0_ref[0], s1_ref[0], s2_ref[0]]
    mx = jnp.maximum(jnp.maximum(st[0], st[1]), st[2])
    ex = [jnp.exp2(s - mx) for s in st]
    inv = 1.0 / (ex[0] + ex[1] + ex[2])
    e = e_ref[...]
    w0 = _dot((ex[0] * inv).astype(BF16), e)
    w1 = _dot((ex[1] * inv).astype(BF16), e)
    w2 = 1.0 - (w0 + w1)
    merged = (w0.astype(BF16) * o0_ref[0] + w1.astype(BF16) * o1_ref[0]
              + w2.astype(BF16) * o2_ref[0])
    act = merged * gate_ref[0]
    y = _dot(act, w_out_ref[...])
    h_new = _deepnorm_ln(h_ref[0], y, g_ref[...], b_ref[...])
    out_ref[0] = h_new
    if plan:
        _project_rows(h_new, w_refs, proj_refs[:-1], proj_refs[-1], plan, tile)


def _merge_layer(outs, stats, gate, h, w_out, ln_g, ln_b, w_next=None, plan=(), *, tile=None):
    B, S, D = h.shape
    tile = tile or (512 if plan else 1024)
    e = _head_expand_matrix()
    row = lambda width: pl.BlockSpec((1, tile, width), lambda b, s: (b, s, 0))
    proj_shape = [jax.ShapeDtypeStruct((B, d, S // d, D_B), BF16) for _, _, d, _ in plan]
    proj_specs = [pl.BlockSpec((1, d, tile // d, D_B), lambda b, s: (b, 0, s, 0)) for _, _, d, _ in plan]
    weights = [w_next] if plan else []
    return pl.pallas_call(
        functools.partial(_merge_kernel, plan=tuple(plan), tile=tile),
        grid=(B, S // tile),
        in_specs=[row(D_B)] * 3 + [row(LANES)] * 3 + [row(D_B), row(D)] + [
            _const_spec(e.shape), _const_spec(w_out.shape),
            _const_spec(ln_g.shape), _const_spec(ln_b.shape)] + [_const_spec(w.shape) for w in weights],
        out_specs=[row(D)] + proj_specs,
        out_shape=[jax.ShapeDtypeStruct((B, S, D), F32)] + proj_shape,
        scratch_shapes=[pltpu.VMEM((D // LANES, tile, LANES), F32)] if plan else [],
        compiler_params=pltpu.CompilerParams(
            dimension_semantics=("arbitrary", "arbitrary"),
            vmem_limit_bytes=VMEM_LIMIT_BYTES),
        name="merge_layer",
    )(*outs, *stats, gate, h, e, w_out, ln_g, ln_b, *weights)


def kernel(x, w_in_a, w_grp_a, scale_a, w_out_a, w_kv, w_in_b, w_out_b, ln_g, ln_b):
    n_q = N_GROUPS * D_B
    B, S, _ = x.shape
    dils = [d for _, d in ATTN_PAIRS]
    q_scale = jnp.concatenate([jnp.full((n_q,), HEAD_DIM ** -0.5 * LOG2_E, F32), jnp.ones((D_B,), F32)])
    def kv_plan(wi):
        return ([(wi, grp * D_B, d, False) for grp, d in enumerate(dils)]
                + [(wi, n_q + grp * D_B, d, False) for grp, d in enumerate(dils)])

    def q_plan(wi):
        return [(wi, grp * D_B, d, False) for grp, d in enumerate(dils)] + [(wi, n_q, 1, True)]

    w_q = [(w_in_b[j] * q_scale).astype(BF16) for j in range(DEPTH - N_A_LAYERS)]
    h = x
    ks = vs = qs = gate = None
    for l in range(DEPTH):
        g = ln_g[l][None, :]
        b = ln_b[l][None, :]
        if l < N_A_LAYERS:
            h = _pool_layer(h, w_in_a[l].astype(BF16), w_grp_a[l].astype(BF16),
                            scale_a[l][None, :], w_out_a[l].astype(BF16), g, b)
            if l == N_A_LAYERS - 1:
                outs = _project(h, [w_kv.astype(BF16), w_q[0]], kv_plan(0) + q_plan(1))
                ks, vs = outs[:N_GROUPS], outs[N_GROUPS:2 * N_GROUPS]
                *qs, gate = outs[2 * N_GROUPS:]
        else:
            j = l - N_A_LAYERS
            outs, stats = zip(*[_attention_group(qs[grp], ks[grp], vs[grp], grp)
                                for grp in range(N_GROUPS)])
            last = l == DEPTH - 1
            h, *nxt = _merge_layer(outs, stats, gate.reshape(B, S, D_B), h, w_out_b[j].astype(BF16), g, b,
                                   *(() if last else (w_q[j + 1], q_plan(0))))
            if not last:
                *qs, gate = nxt
    return h
```

```python
import functools

import jax
import jax.numpy as jnp
import numpy as np
from jax import lax
from jax.experimental import pallas as pl
from jax.experimental.pallas import tpu as pltpu

D_MODEL = 1024
DEPTH = 4
N_A_LAYERS = DEPTH // 2
D_A = 2 * D_MODEL
POOL_WINDOWS = (2, 4, 8, 16)
N_POOL_GROUPS = len(POOL_WINDOWS)
POOL_GW = D_A // N_POOL_GROUPS
ATTN_PAIRS = ((128, 1), (512, 4), (2048, 16))
N_GROUPS = len(ATTN_PAIRS)
N_HEADS = 16
HEAD_DIM = D_MODEL // N_HEADS
D_B = N_HEADS * HEAD_DIM
BLK = 128
DN_ALPHA = (2 * DEPTH) ** 0.25
LN_EPS = 1e-5
NEG_INF = -1e30
LOG2_E = 1.4426950408889634

LANES = 128
N_PAIRS = N_HEADS // 2
SUBLANES = 8
POOL_HALO = SUBLANES * N_POOL_GROUPS
VMEM_LIMIT_BYTES = 56 * 1024 * 1024
ATTN_UNROLL = 8

F32 = jnp.float32
BF16 = jnp.bfloat16


def _dot(a, b):
    return jnp.dot(a, b, preferred_element_type=F32)


def _silu(x):
    hx = 0.5 * x
    return hx + hx * jnp.tanh(hx)


def _deepnorm_ln(h, y, g, b):
    z = DN_ALPHA * h + y
    mu = jnp.mean(z, axis=-1, keepdims=True)
    zc = z - mu
    var = jnp.mean(zc * zc, axis=-1, keepdims=True)
    return zc * lax.rsqrt(var + LN_EPS) * g + b


def _const_spec(shape):
    nd = len(shape)
    return pl.BlockSpec(shape, lambda *_: (0,) * nd, pipeline_mode=pl.Buffered(1))


def _fold_kernel(w_u_ref, w_grp_ref, scale_ref, o_ref):
    o_ref[...] = (_dot(w_u_ref[...], w_grp_ref[0]) * scale_ref[...]).astype(BF16)


def _fold_pool_weights(w_in, w_grp, scale):
    D = w_in.shape[0]
    return pl.pallas_call(
        _fold_kernel,
        grid=(N_POOL_GROUPS,),
        in_specs=[
            pl.BlockSpec((D, POOL_GW), lambda g: (0, g)),
            pl.BlockSpec((1, POOL_GW, POOL_GW), lambda g: (g, 0, 0)),
            pl.BlockSpec((1, POOL_GW), lambda g: (0, g)),
        ],
        out_specs=pl.BlockSpec((D, POOL_GW), lambda g: (0, g)),
        out_shape=jax.ShapeDtypeStruct((D, D_A), BF16),
        compiler_params=pltpu.CompilerParams(dimension_semantics=("arbitrary",)),
        name="fold_pool_weights",
    )(w_in, w_grp, scale)


def _pool_layer_kernel(h_ref, w_mix_ref, w_gate_ref, w_out_ref, g_ref, b_ref,
                       o_ref, hext_ref, *sum_refs, tile):
    s = pl.program_id(1)
    halo, end = POOL_HALO, POOL_HALO + tile

    @pl.when(s == 0)
    def _():
        hext_ref[0:halo, :] = jnp.zeros((halo, D_MODEL), F32)

    h = h_ref[0]
    hb = h.astype(BF16)
    gates = [_dot(hb, w_gate_ref[:, g * POOL_GW:(g + 1) * POOL_GW]) for g in range(N_POOL_GROUPS)]
    hext_ref[halo:end, :] = h
    src, w, lo = hext_ref, 1, 0
    window_sums = []
    for dst in sum_refs:
        lo += SUBLANES
        dst[lo:end, :] = src[lo:end, :] + src[lo - w:end - w, :]
        window_sums.append(dst[halo:end, :])
        src, w = dst, 2 * w
    window_sums.append(src[halo:end, :] + src[halo - w:end - w, :])

    pos = s * tile + lax.broadcasted_iota(jnp.int32, (tile, 1), 0)
    y = jnp.zeros((tile, D_MODEL), F32)
    for g, (w, wsum) in enumerate(zip(POOL_WINDOWS, window_sums)):
        c0 = g * POOL_GW
        inv_cnt = 1.0 / jnp.minimum(pos + 1, w).astype(F32)
        lhs = (wsum * inv_cnt - h).astype(BF16)
        mixed = _dot(lhs, w_mix_ref[:, c0:c0 + POOL_GW])
        act = mixed * _silu(gates[g])
        y = y + _dot(act.astype(BF16), w_out_ref[c0:c0 + POOL_GW, :])
    hext_ref[0:halo, :] = hext_ref[tile:end, :]
    o_ref[0] = _deepnorm_ln(h, y, g_ref[...], b_ref[...])


def _pool_layer(h, w_in, w_grp, scale, w_out, ln_g, ln_b, *, tile=512):
    B, S, D = h.shape
    w_mix = _fold_pool_weights(w_in, w_grp, scale)
    kern = functools.partial(_pool_layer_kernel, tile=tile)
    return pl.pallas_call(
        kern,
        grid=(B, S // tile),
        in_specs=[
            pl.BlockSpec((1, tile, D), lambda b, s: (b, s, 0)),
            _const_spec(w_mix.shape),
            pl.BlockSpec((D, D_A), lambda b, s: (0, 1), pipeline_mode=pl.Buffered(1)),
            _const_spec(w_out.shape),
            _const_spec(ln_g.shape),
            _const_spec(ln_b.shape),
        ],
        out_specs=pl.BlockSpec((1, tile, D), lambda b, s: (b, s, 0)),
        out_shape=jax.ShapeDtypeStruct((B, S, D), F32),
        scratch_shapes=[pltpu.VMEM((POOL_HALO + tile, D), F32)] * N_POOL_GROUPS,
        compiler_params=pltpu.CompilerParams(
            dimension_semantics=("arbitrary", "arbitrary"),
            vmem_limit_bytes=VMEM_LIMIT_BYTES),
        name="pool_layer",
    )(h, w_mix, w_in, w_out, ln_g, ln_b)


def _proj_kernel(x_ref, *refs, plan, tile, n_weights):
    w_refs = refs[:n_weights]
    out_refs, slab_ref = refs[n_weights:n_weights + len(plan)], refs[n_weights + len(plan)]
    _project_rows(x_ref[0], w_refs, out_refs, slab_ref, plan, tile)


def _project_rows(x, w_refs, out_refs, slab_ref, plan, tile):
    lhs = {1: x.astype(BF16)}
    dils = sorted({d for _, _, d, _ in plan if d > 1})
    if dils:
        for j in range(D_MODEL // LANES):
            slab_ref[j] = x[:, j * LANES:(j + 1) * LANES]
    for d in dils:
        planes = [
            jnp.concatenate([slab_ref[j, pl.ds(r, tile // d, stride=d), :]
                             for j in range(D_MODEL // LANES)], axis=1)
            for r in range(d)]
        lhs[d] = jnp.concatenate(planes, axis=0).astype(BF16)
    for (wi, c0, d, gated), o_ref in zip(plan, out_refs):
        res = _dot(lhs[d], w_refs[wi][:, c0:c0 + D_B])
        res = (_silu(res) if gated else res).astype(o_ref.dtype)
        rows = tile // d
        for r in range(d):
            o_ref[0, r] = res[r * rows:(r + 1) * rows]


def _project(x, weights, plan, *, tile=512):
    B, S, D = x.shape
    out_shape = [jax.ShapeDtypeStruct((B, d, S // d, D_B), BF16) for _, _, d, _ in plan]
    out_specs = [pl.BlockSpec((1, d, tile // d, D_B), lambda b, s: (b, 0, s, 0)) for _, _, d, _ in plan]
    kern = functools.partial(_proj_kernel, plan=tuple(plan), tile=tile, n_weights=len(weights))
    return pl.pallas_call(
        kern,
        grid=(B, S // tile),
        in_specs=[pl.BlockSpec((1, tile, D), lambda b, s: (b, s, 0))]
                 + [_const_spec(w.shape) for w in weights],
        out_specs=out_specs,
        out_shape=out_shape,
        scratch_shapes=[pltpu.VMEM((D // LANES, tile, LANES), F32)],
        compiler_params=pltpu.CompilerParams(
            dimension_semantics=("arbitrary", "arbitrary"),
            vmem_limit_bytes=VMEM_LIMIT_BYTES),
        name="project",
    )(x, *weights)


def _attn_bias(group, dilation):
    n = N_GROUPS * N_HEADS
    i = np.arange(1, n + 1, dtype=np.float32)
    slopes = np.exp2(np.float32(-8.0) * i / np.float32(n)).reshape(N_GROUPS, N_HEADS)[group]
    a = np.arange(BLK)[:, None]
    c = np.arange(2 * BLK)[None, :]
    rel = BLK + a - c
    win_sub = ATTN_PAIRS[group][0] // dilation
    valid = (rel >= 0) & (rel <= win_sub)
    bias = (-slopes[:, None, None] * (dilation * rel).astype(np.float32)[None]
            * np.float32(LOG2_E)).astype(np.float32)
    later = np.where(valid[None], bias, np.float32(NEG_INF))
    first = np.where((valid & (c >= BLK))[None], bias, np.float32(NEG_INF))
    table = np.stack([later, first], axis=0).reshape(2, N_PAIRS, 2 * BLK, 2 * BLK)
    return jnp.asarray(table, dtype=F32)


def _attn_kernel(q_ref, kp_ref, kc_ref, vp_ref, vc_ref, bias_ref, o_ref, st_ref, *scratch,
                 dilation, res_per_step, blk_per_step):
    n = pl.program_id(1)
    rg = pl.program_id(2)
    lane = lax.broadcasted_iota(jnp.int32, (1, LANES), 1)
    lo = lane < HEAD_DIM
    ones = jnp.ones((2 * BLK, LANES), BF16)

    def block(rr, j, key_window, bias_tile):
        row0 = j * BLK if isinstance(j, int) else pl.multiple_of(j * BLK, BLK)
        out0 = row0 * dilation + rg * res_per_step + rr
        stats = jnp.zeros((BLK, LANES), F32)
        for p in range(N_PAIRS):
            cols = slice(p * LANES, (p + 1) * LANES)
            qp = q_ref[0, rr, pl.ds(row0, BLK), cols]
            kp = key_window(kp_ref, kc_ref, cols)
            vp = key_window(vp_ref, vc_ref, cols)
            zero = jnp.zeros_like(qp)
            q2 = jnp.concatenate([jnp.where(lo, qp, zero), jnp.where(lo, zero, qp)], axis=0)
            sc = lax.dot_general(q2, kp, (((1,), (1,)), ((), ())), preferred_element_type=F32)
            sc = sc + bias_tile(p)
            m = jnp.max(sc, axis=-1, keepdims=True)
            pe = jnp.exp2((sc - m).astype(BF16))
            acc = _dot(pe, jnp.concatenate([vp, ones], axis=1))
            pick = lambda x: jnp.where(lo, x[:BLK], x[BLK:])
            denom = pick(acc[:, LANES:])
            m_pair = pick(m)
            o = pick(acc[:, :LANES]) / denom
            if dilation == 1:
                o_ref[0, pl.ds(row0, BLK), cols] = o.astype(BF16)
            else:
                scratch[0][p, pl.ds(out0, BLK, stride=dilation), :] = o
            lse = m_pair + jnp.log2(denom)
            stats = jnp.where((lane & (HEAD_DIM - 1)) == p, lse, stats)
        if dilation == 1:
            st_ref[0, pl.ds(row0, BLK), :] = stats
        else:
            st_ref[0, pl.ds(out0, BLK, stride=dilation), :] = stats

    first_tile = (n == 0).astype(jnp.int32)

    def first_block(rr, carry):
        window = lambda prev, cur, cols: jnp.concatenate(
            [prev[0, rr, :, cols], cur[0, rr, 0:BLK, cols]], axis=0)
        block(rr, 0, window, lambda p: bias_ref[first_tile, p])
        return carry

    lax.fori_loop(0, res_per_step, first_block, 0, unroll=min(ATTN_UNROLL, res_per_step))

    rest = blk_per_step - 1
    if rest:
        def later_block(i, carry):
            rr, j = i // rest, 1 + i % rest
            k0 = pl.multiple_of((j - 1) * BLK, BLK)
            window = lambda prev, cur, cols: cur[0, rr, pl.ds(k0, 2 * BLK), cols]
            block(rr, j, window, lambda p: bias_ref[0, p])
            return carry

        lax.fori_loop(0, res_per_step * rest, later_block, 0, unroll=ATTN_UNROLL)

    if dilation > 1:
        @pl.when(rg == dilation // res_per_step - 1)
        def _():
            for p in range(N_PAIRS):
                o_ref[0, :, p * LANES:(p + 1) * LANES] = scratch[0][p].astype(BF16)


ATTN_STEP_SHAPE = {1: (1, 8), 4: (4, 2), 16: (8, 1)}


def _attention_group(q, k, v, group):
    _, d = ATTN_PAIRS[group]
    B, _, L, _ = q.shape
    S = L * d
    rb, qb = ATTN_STEP_SHAPE[d]
    rows = qb * BLK
    bias = _attn_bias(group, d)
    cur_spec = pl.BlockSpec((1, rb, rows, D_B), lambda b, n, rg: (b, rg, n, 0))
    prev_spec = pl.BlockSpec((1, rb, BLK, D_B), lambda b, n, rg: (b, rg, jnp.maximum(n * qb - 1, 0), 0))
    kern = functools.partial(_attn_kernel, dilation=d, res_per_step=rb, blk_per_step=qb)
    return pl.pallas_call(
        kern,
        grid=(B, L // rows, d // rb),
        in_specs=[cur_spec, prev_spec, cur_spec, prev_spec, cur_spec, _const_spec(bias.shape)],
        out_specs=[
            pl.BlockSpec((1, rows * d, D_B), lambda b, n, rg: (b, n, 0)),
            pl.BlockSpec((1, rows * d, LANES), lambda b, n, rg: (b, n, 0)),
        ],
        out_shape=[
            jax.ShapeDtypeStruct((B, S, D_B), BF16),
            jax.ShapeDtypeStruct((B, S, LANES), F32),
        ],
        scratch_shapes=[pltpu.VMEM((N_PAIRS, rows * d, LANES), F32)] if d > 1 else [],
        compiler_params=pltpu.CompilerParams(
            dimension_semantics=("arbitrary", "arbitrary", "arbitrary"),
            vmem_limit_bytes=VMEM_LIMIT_BYTES),
        name=f"attn_g{group}",
    )(q, k, k, v, v, bias)


def _head_expand_matrix():
    row = np.arange(LANES)[:, None]
    col = np.arange(D_B)[None, :]
    head = col // HEAD_DIM
    src = (head // 2) + (head % 2) * HEAD_DIM
    return jnp.asarray((row == src).astype(np.float32), dtype=BF16)


def _merge_kernel(o0_ref, o1_ref, o2_ref, s0_ref, s1_ref, s2_ref, gate_ref, h_ref,
                  e_ref, w_out_ref, g_ref, b_ref, *refs, plan, tile):
    w_refs, (out_ref, *proj_refs) = (refs[:1], refs[1:]) if plan else ((), refs)
    st = [s0_ref[0], s1_ref[0], s2_ref[0]]
    mx = jnp.maximum(jnp.maximum(st[0], st[1]), st[2])
    ex = [jnp.exp2(s - mx) for s in st]
    inv = 1.0 / (ex[0] + ex[1] + ex[2])
    e = e_ref[...]
    w0 = _dot((ex[0] * inv).astype(BF16), e)
    w1 = _dot((ex[1] * inv).astype(BF16), e)
    w2 = 1.0 - (w0 + w1)
    merged = (w0.astype(BF16) * o0_ref[0] + w1.astype(BF16) * o1_ref[0]
              + w2.astype(BF16) * o2_ref[0])
    act = merged * gate_ref[0]
    y = _dot(act, w_out_ref[...])
    h_new = _deepnorm_ln(h_ref[0], y, g_ref[...], b_ref[...])
    out_ref[0] = h_new
    if plan:
        _project_rows(h_new, w_refs, proj_refs[:-1], proj_refs[-1], plan, tile)


def _merge_layer(outs, stats, gate, h, w_out, ln_g, ln_b, w_next=None, plan=(), *, tile=None):
    B, S, D = h.shape
    tile = tile or (512 if plan else 1024)
    e = _head_expand_matrix()
    row = lambda width: pl.BlockSpec((1, tile, width), lambda b, s: (b, s, 0))
    proj_shape = [jax.ShapeDtypeStruct((B, d, S // d, D_B), BF16) for _, _, d, _ in plan]
    proj_specs = [pl.BlockSpec((1, d, tile // d, D_B), lambda b, s: (b, 0, s, 0)) for _, _, d, _ in plan]
    weights = [w_next] if plan else []
    return pl.pallas_call(
        functools.partial(_merge_kernel, plan=tuple(plan), tile=tile),
        grid=(B, S // tile),
        in_specs=[row(D_B)] * 3 + [row(LANES)] * 3 + [row(D_B), row(D)] + [
            _const_spec(e.shape), _const_spec(w_out.shape),
            _const_spec(ln_g.shape), _const_spec(ln_b.shape)] + [_const_spec(w.shape) for w in weights],
        out_specs=[row(D)] + proj_specs,
        out_shape=[jax.ShapeDtypeStruct((B, S, D), F32)] + proj_shape,
        scratch_shapes=[pltpu.VMEM((D // LANES, tile, LANES), F32)] if plan else [],
        compiler_params=pltpu.CompilerParams(
            dimension_semantics=("arbitrary", "arbitrary"),
            vmem_limit_bytes=VMEM_LIMIT_BYTES),
        name="merge_layer",
    )(*outs, *stats, gate, h, e, w_out, ln_g, ln_b, *weights)


def kernel(x, w_in_a, w_grp_a, scale_a, w_out_a, w_kv, w_in_b, w_out_b, ln_g, ln_b):
    n_q = N_GROUPS * D_B
    B, S, _ = x.shape
    dils = [d for _, d in ATTN_PAIRS]
    q_scale = jnp.concatenate([jnp.full((n_q,), HEAD_DIM ** -0.5 * LOG2_E, F32), jnp.ones((D_B,), F32)])
    def kv_plan(wi):
        return ([(wi, grp * D_B, d, False) for grp, d in enumerate(dils)]
                + [(wi, n_q + grp * D_B, d, False) for grp, d in enumerate(dils)])

    def q_plan(wi):
        return [(wi, grp * D_B, d, False) for grp, d in enumerate(dils)] + [(wi, n_q, 1, True)]

    w_q = [(w_in_b[j] * q_scale).astype(BF16) for j in range(DEPTH - N_A_LAYERS)]
    h = x
    ks = vs = qs = gate = None
    for l in range(DEPTH):
        g = ln_g[l][None, :]
        b = ln_b[l][None, :]
        if l < N_A_LAYERS:
            h = _pool_layer(h, w_in_a[l].astype(BF16), w_grp_a[l].astype(BF16),
                            scale_a[l][None, :], w_out_a[l].astype(BF16), g, b)
            if l == N_A_LAYERS - 1:
                outs = _project(h, [w_kv.astype(BF16), w_q[0]], kv_plan(0) + q_plan(1))
                ks, vs = outs[:N_GROUPS], outs[N_GROUPS:2 * N_GROUPS]
                *qs, gate = outs[2 * N_GROUPS:]
        else:
            j = l - N_A_LAYERS
            outs, stats = zip(*[_attention_group(qs[grp], ks[grp], vs[grp], grp)
                                for grp in range(N_GROUPS)])
            last = l == DEPTH - 1
            h, *nxt = _merge_layer(outs, stats, gate.reshape(B, S, D_B), h, w_out_b[j].astype(BF16), g, b,
                                   *(() if last else (w_q[j + 1], q_plan(0))))
            if not last:
                *qs, gate = nxt
    return h
```

```python
import functools

import jax
import jax.numpy as jnp
import numpy as np
from jax import lax
from jax.experimental import pallas as pl
from jax.experimental.pallas import tpu as pltpu

D_MODEL = 1024
DEPTH = 4
N_A_LAYERS = DEPTH // 2
D_A = 2 * D_MODEL
POOL_WINDOWS = (2, 4, 8, 16)
N_POOL_GROUPS = len(POOL_WINDOWS)
POOL_GW = D_A // N_POOL_GROUPS
ATTN_PAIRS = ((128, 1), (512, 4), (2048, 16))
N_GROUPS = len(ATTN_PAIRS)
N_HEADS = 16
HEAD_DIM = D_MODEL // N_HEADS
D_B = N_HEADS * HEAD_DIM
BLK = 128
DN_ALPHA = (2 * DEPTH) ** 0.25
LN_EPS = 1e-5
NEG_INF = -1e30
LOG2_E = 1.4426950408889634

LANES = 128
N_PAIRS = N_HEADS // 2
SUBLANES = 8
POOL_HALO = SUBLANES * N_POOL_GROUPS
VMEM_LIMIT_BYTES = 56 * 1024 * 1024
VMEM_TEMP_FACTOR = 1.5
ATTN_UNROLL = 8

F32 = jnp.float32
BF16 = jnp.bfloat16


def _dot(a, b):
    return jnp.dot(a, b, preferred_element_type=F32)


def _silu(x):
    hx = 0.5 * x
    return hx + hx * jnp.tanh(hx)


def _deepnorm_ln(h, y, g, b):
    z = DN_ALPHA * h + y
    mu = jnp.mean(z, axis=-1, keepdims=True)
    zc = z - mu
    var = jnp.mean(zc * zc, axis=-1, keepdims=True)
    return zc * lax.rsqrt(var + LN_EPS) * g + b


def _nbytes(shape, dtype):
    return int(np.prod(shape)) * jnp.dtype(dtype).itemsize


def _vmem_limit(declared_bytes):
    return min(VMEM_LIMIT_BYTES, int(VMEM_TEMP_FACTOR * declared_bytes))


def _const_spec(shape):
    nd = len(shape)
    return pl.BlockSpec(shape, lambda *_: (0,) * nd, pipeline_mode=pl.Buffered(1))


def _fold_kernel(w_u_ref, w_grp_ref, scale_ref, o_ref):
    o_ref[...] = (_dot(w_u_ref[...], w_grp_ref[0]) * scale_ref[...]).astype(BF16)


def _fold_pool_weights(w_in, w_grp, scale):
    D = w_in.shape[0]
    return pl.pallas_call(
        _fold_kernel,
        grid=(N_POOL_GROUPS,),
        in_specs=[
            pl.BlockSpec((D, POOL_GW), lambda g: (0, g)),
            pl.BlockSpec((1, POOL_GW, POOL_GW), lambda g: (g, 0, 0)),
            pl.BlockSpec((1, POOL_GW), lambda g: (0, g)),
        ],
        out_specs=pl.BlockSpec((D, POOL_GW), lambda g: (0, g)),
        out_shape=jax.ShapeDtypeStruct((D, D_A), BF16),
        compiler_params=pltpu.CompilerParams(dimension_semantics=("arbitrary",)),
        name="fold_pool_weights",
    )(w_in, w_grp, scale)


def _pool_layer_kernel(h_ref, w_mix_ref, w_gate_ref, w_out_ref, g_ref, b_ref,
                       o_ref, hext_ref, *sum_refs, tile):
    s = pl.program_id(1)
    halo, end = POOL_HALO, POOL_HALO + tile

    @pl.when(s == 0)
    def _():
        hext_ref[0:halo, :] = jnp.zeros((halo, D_MODEL), F32)

    h = h_ref[0]
    hb = h.astype(BF16)
    gates = [_dot(hb, w_gate_ref[:, g * POOL_GW:(g + 1) * POOL_GW]) for g in range(N_POOL_GROUPS)]
    hext_ref[halo:end, :] = h
    src, w, lo = hext_ref, 1, 0
    window_sums = []
    for dst in sum_refs:
        lo += SUBLANES
        dst[lo:end, :] = src[lo:end, :] + src[lo - w:end - w, :]
        window_sums.append(dst[halo:end, :])
        src, w = dst, 2 * w
    window_sums.append(src[halo:end, :] + src[halo - w:end - w, :])

    pos = s * tile + lax.broadcasted_iota(jnp.int32, (tile, 1), 0)
    y = jnp.zeros((tile, D_MODEL), F32)
    for g, (w, wsum) in enumerate(zip(POOL_WINDOWS, window_sums)):
        c0 = g * POOL_GW
        inv_cnt = 1.0 / jnp.minimum(pos + 1, w).astype(F32)
        lhs = (wsum * inv_cnt - h).astype(BF16)
        mixed = _dot(lhs, w_mix_ref[:, c0:c0 + POOL_GW])
        act = mixed * _silu(gates[g])
        y = y + _dot(act.astype(BF16), w_out_ref[c0:c0 + POOL_GW, :])
    hext_ref[0:halo, :] = hext_ref[tile:end, :]
    o_ref[0] = _deepnorm_ln(h, y, g_ref[...], b_ref[...])


def _pool_layer(h, w_in, w_grp, scale, w_out, ln_g, ln_b, *, tile=512):
    B, S, D = h.shape
    w_mix = _fold_pool_weights(w_in, w_grp, scale)
    kern = functools.partial(_pool_layer_kernel, tile=tile)
    declared = (4 * _nbytes((tile, D), F32) + _nbytes(w_mix.shape, BF16) + _nbytes((D, D_A), BF16)
                + _nbytes(w_out.shape, BF16) + N_POOL_GROUPS * _nbytes((POOL_HALO + tile, D), F32))
    return pl.pallas_call(
        kern,
        grid=(B, S // tile),
        in_specs=[
            pl.BlockSpec((1, tile, D), lambda b, s: (b, s, 0)),
            _const_spec(w_mix.shape),
            pl.BlockSpec((D, D_A), lambda b, s: (0, 1), pipeline_mode=pl.Buffered(1)),
            _const_spec(w_out.shape),
            _const_spec(ln_g.shape),
            _const_spec(ln_b.shape),
        ],
        out_specs=pl.BlockSpec((1, tile, D), lambda b, s: (b, s, 0)),
        out_shape=jax.ShapeDtypeStruct((B, S, D), F32),
        scratch_shapes=[pltpu.VMEM((POOL_HALO + tile, D), F32)] * N_POOL_GROUPS,
        compiler_params=pltpu.CompilerParams(
            dimension_semantics=("arbitrary", "arbitrary"),
            vmem_limit_bytes=_vmem_limit(declared)),
        name="pool_layer",
    )(h, w_mix, w_in, w_out, ln_g, ln_b)


def _proj_kernel(x_ref, *refs, plan, tile, n_weights):
    w_refs = refs[:n_weights]
    out_refs, slab_ref = refs[n_weights:n_weights + len(plan)], refs[n_weights + len(plan)]
    _project_rows(x_ref[0], w_refs, out_refs, slab_ref, plan, tile)


def _project_rows(x, w_refs, out_refs, slab_ref, plan, tile):
    lhs = {1: x.astype(BF16)}
    dils = sorted({d for _, _, d, _ in plan if d > 1})
    if dils:
        for j in range(D_MODEL // LANES):
            slab_ref[j] = x[:, j * LANES:(j + 1) * LANES]
    for d in dils:
        planes = [
            jnp.concatenate([slab_ref[j, pl.ds(r, tile // d, stride=d), :]
                             for j in range(D_MODEL // LANES)], axis=1)
            for r in range(d)]
        lhs[d] = jnp.concatenate(planes, axis=0).astype(BF16)
    for (wi, c0, d, gated), o_ref in zip(plan, out_refs):
        res = _dot(lhs[d], w_refs[wi][:, c0:c0 + D_B])
        res = (_silu(res) if gated else res).astype(o_ref.dtype)
        rows = tile // d
        for r in range(d):
            o_ref[0, r] = res[r * rows:(r + 1) * rows]


def _project(x, weights, plan, *, tile=512):
    B, S, D = x.shape
    out_shape = [jax.ShapeDtypeStruct((B, d, S // d, D_B), BF16) for _, _, d, _ in plan]
    out_specs = [pl.BlockSpec((1, d, tile // d, D_B), lambda b, s: (b, 0, s, 0)) for _, _, d, _ in plan]
    kern = functools.partial(_proj_kernel, plan=tuple(plan), tile=tile, n_weights=len(weights))
    return pl.pallas_call(
        kern,
        grid=(B, S // tile),
        in_specs=[pl.BlockSpec((1, tile, D), lambda b, s: (b, s, 0))]
                 + [_const_spec(w.shape) for w in weights],
        out_specs=out_specs,
        out_shape=out_shape,
        scratch_shapes=[pltpu.VMEM((D // LANES, tile, LANES), F32)],
        compiler_params=pltpu.CompilerParams(
            dimension_semantics=("arbitrary", "arbitrary"),
            vmem_limit_bytes=VMEM_LIMIT_BYTES),
        name="project",
    )(x, *weights)


def _attn_bias(group, dilation):
    n = N_GROUPS * N_HEADS
    i = np.arange(1, n + 1, dtype=np.float32)
    slopes = np.exp2(np.float32(-8.0) * i / np.float32(n)).reshape(N_GROUPS, N_HEADS)[group]
    a = np.arange(BLK)[:, None]
    c = np.arange(2 * BLK)[None, :]
    rel = BLK + a - c
    win_sub = ATTN_PAIRS[group][0] // dilation
    valid = (rel >= 0) & (rel <= win_sub)
    bias = (-slopes[:, None, None] * (dilation * rel).astype(np.float32)[None]
            * np.float32(LOG2_E)).astype(np.float32)
    later = np.where(valid[None], bias, np.float32(NEG_INF))
    first = np.where((valid & (c >= BLK))[None], bias, np.float32(NEG_INF))
    table = np.stack([later, first], axis=0).reshape(2, N_PAIRS, 2 * BLK, 2 * BLK)
    return jnp.asarray(table, dtype=F32)


def _attn_kernel(q_ref, kp_ref, kc_ref, vp_ref, vc_ref, bias_ref, o_ref, st_ref, *scratch,
                 dilation, res_per_step, blk_per_step):
    n = pl.program_id(1)
    rg = pl.program_id(2)
    lane = lax.broadcasted_iota(jnp.int32, (1, LANES), 1)
    lo = lane < HEAD_DIM
    ones = jnp.ones((2 * BLK, LANES), BF16)

    def block(rr, j, key_window, bias_tile):
        row0 = j * BLK if isinstance(j, int) else pl.multiple_of(j * BLK, BLK)
        out0 = row0 * dilation + rg * res_per_step + rr
        stats = jnp.zeros((BLK, LANES), F32)
        for p in range(N_PAIRS):
            cols = slice(p * LANES, (p + 1) * LANES)
            qp = q_ref[0, rr, pl.ds(row0, BLK), cols]
            kp = key_window(kp_ref, kc_ref, cols)
            vp = key_window(vp_ref, vc_ref, cols)
            zero = jnp.zeros_like(qp)
            q2 = jnp.concatenate([jnp.where(lo, qp, zero), jnp.where(lo, zero, qp)], axis=0)
            sc = lax.dot_general(q2, kp, (((1,), (1,)), ((), ())), preferred_element_type=F32)
            sc = sc + bias_tile(p)
            m = jnp.max(sc, axis=-1, keepdims=True)
            pe = jnp.exp2((sc - m).astype(BF16))
            acc = _dot(pe, jnp.concatenate([vp, ones], axis=1))
            pick = lambda x: jnp.where(lo, x[:BLK], x[BLK:])
            denom = pick(acc[:, LANES:])
            m_pair = pick(m)
            o = pick(acc[:, :LANES]) / denom
            if dilation == 1:
                o_ref[0, pl.ds(row0, BLK), cols] = o.astype(BF16)
            else:
                scratch[0][p, pl.ds(out0, BLK, stride=dilation), :] = o
            lse = m_pair + jnp.log2(denom)
            stats = jnp.where((lane & (HEAD_DIM - 1)) == p, lse, stats)
        if dilation == 1:
            st_ref[0, pl.ds(row0, BLK), :] = stats
        else:
            st_ref[0, pl.ds(out0, BLK, stride=dilation), :] = stats

    first_tile = (n == 0).astype(jnp.int32)

    def first_block(rr, carry):
        window = lambda prev, cur, cols: jnp.concatenate(
            [prev[0, rr, :, cols], cur[0, rr, 0:BLK, cols]], axis=0)
        block(rr, 0, window, lambda p: bias_ref[first_tile, p])
        return carry

    lax.fori_loop(0, res_per_step, first_block, 0, unroll=min(ATTN_UNROLL, res_per_step))

    rest = blk_per_step - 1
    if rest:
        def later_block(i, carry):
            rr, j = i // rest, 1 + i % rest
            k0 = pl.multiple_of((j - 1) * BLK, BLK)
            window = lambda prev, cur, cols: cur[0, rr, pl.ds(k0, 2 * BLK), cols]
            block(rr, j, window, lambda p: bias_ref[0, p])
            return carry

        lax.fori_loop(0, res_per_step * rest, later_block, 0, unroll=ATTN_UNROLL)

    if dilation > 1:
        @pl.when(rg == dilation // res_per_step - 1)
        def _():
            for p in range(N_PAIRS):
                o_ref[0, :, p * LANES:(p + 1) * LANES] = scratch[0][p].astype(BF16)


ATTN_STEP_SHAPE = {1: (1, 8), 4: (4, 2), 16: (8, 1)}


def _attention_group(q, k, v, group):
    _, d = ATTN_PAIRS[group]
    B, _, L, _ = q.shape
    S = L * d
    rb, qb = ATTN_STEP_SHAPE[d]
    rows = qb * BLK
    bias = _attn_bias(group, d)
    cur_spec = pl.BlockSpec((1, rb, rows, D_B), lambda b, n, rg: (b, rg, n, 0))
    prev_spec = pl.BlockSpec((1, rb, BLK, D_B), lambda b, n, rg: (b, rg, jnp.maximum(n * qb - 1, 0), 0))
    kern = functools.partial(_attn_kernel, dilation=d, res_per_step=rb, blk_per_step=qb)
    declared = (2 * 3 * _nbytes((rb, rows, D_B), BF16) + 2 * 2 * _nbytes((rb, BLK, D_B), BF16)
                + _nbytes(bias.shape, F32) + 2 * _nbytes((rows * d, D_B), BF16)
                + 2 * _nbytes((rows * d, LANES), F32)
                + (_nbytes((N_PAIRS, rows * d, LANES), F32) if d > 1 else 0))
    return pl.pallas_call(
        kern,
        grid=(B, L // rows, d // rb),
        in_specs=[cur_spec, prev_spec, cur_spec, prev_spec, cur_spec, _const_spec(bias.shape)],
        out_specs=[
            pl.BlockSpec((1, rows * d, D_B), lambda b, n, rg: (b, n, 0)),
            pl.BlockSpec((1, rows * d, LANES), lambda b, n, rg: (b, n, 0)),
        ],
        out_shape=[
            jax.ShapeDtypeStruct((B, S, D_B), BF16),
            jax.ShapeDtypeStruct((B, S, LANES), F32),
        ],
        scratch_shapes=[pltpu.VMEM((N_PAIRS, rows * d, LANES), F32)] if d > 1 else [],
        compiler_params=pltpu.CompilerParams(
            dimension_semantics=("arbitrary", "arbitrary", "arbitrary"),
            vmem_limit_bytes=_vmem_limit(declared)),
        name=f"attn_g{group}",
    )(q, k, k, v, v, bias)


def _head_expand_matrix():
    row = np.arange(LANES)[:, None]
    col = np.arange(D_B)[None, :]
    head = col // HEAD_DIM
    src = (head // 2) + (head % 2) * HEAD_DIM
    return jnp.asarray((row == src).astype(np.float32), dtype=BF16)


def _merge_kernel(o0_ref, o1_ref, o2_ref, s0_ref, s1_ref, s2_ref, gate_ref, h_ref,
                  e_ref, w_out_ref, g_ref, b_ref, *refs, plan, tile):
    w_refs, (out_ref, *proj_refs) = (refs[:1], refs[1:]) if plan else ((), refs)
    st = [s0_ref[0], s1_ref[0], s2_ref[0]]
    mx = jnp.maximum(jnp.maximum(st[0], st[1]), st[2])
    ex = [jnp.exp2(s - mx) for s in st]
    inv = 1.0 / (ex[0] + ex[1] + ex[2])
    e = e_ref[...]
    w0 = _dot((ex[0] * inv).astype(BF16), e)
    w1 = _dot((ex[1] * inv).astype(BF16), e)
    w2 = 1.0 - (w0 + w1)
    merged = (w0.astype(BF16) * o0_ref[0] + w1.astype(BF16) * o1_ref[0]
              + w2.astype(BF16) * o2_ref[0])
    act = merged * gate_ref[0]
    y = _dot(act, w_out_ref[...])
    h_new = _deepnorm_ln(h_ref[0], y, g_ref[...], b_ref[...])
    out_ref[0] = h_new
    if plan:
        _project_rows(h_new, w_refs, proj_refs[:-1], proj_refs[-1], plan, tile)


def _merge_layer(outs, stats, gate, h, w_out, ln_g, ln_b, w_next=None, plan=(), *, tile=None):
    B, S, D = h.shape
    tile = tile or (512 if plan else 1024)
    e = _head_expand_matrix()
    row = lambda width: pl.BlockSpec((1, tile, width), lambda b, s: (b, s, 0))
    proj_shape = [jax.ShapeDtypeStruct((B, d, S // d, D_B), BF16) for _, _, d, _ in plan]
    proj_specs = [pl.BlockSpec((1, d, tile // d, D_B), lambda b, s: (b, 0, s, 0)) for _, _, d, _ in plan]
    weights = [w_next] if plan else []
    return pl.pallas_call(
        functools.partial(_merge_kernel, plan=tuple(plan), tile=tile),
        grid=(B, S // tile),
        in_specs=[row(D_B)] * 3 + [row(LANES)] * 3 + [row(D_B), row(D)] + [
            _const_spec(e.shape), _const_spec(w_out.shape),
            _const_spec(ln_g.shape), _const_spec(ln_b.shape)] + [_const_spec(w.shape) for w in weights],
        out_specs=[row(D)] + proj_specs,
        out_shape=[jax.ShapeDtypeStruct((B, S, D), F32)] + proj_shape,
        scratch_shapes=[pltpu.VMEM((D // LANES, tile, LANES), F32)] if plan else [],
        compiler_params=pltpu.CompilerParams(
            dimension_semantics=("arbitrary", "arbitrary"),
            vmem_limit_bytes=VMEM_LIMIT_BYTES),
        name="merge_layer",
    )(*outs, *stats, gate, h, e, w_out, ln_g, ln_b, *weights)


def kernel(x, w_in_a, w_grp_a, scale_a, w_out_a, w_kv, w_in_b, w_out_b, ln_g, ln_b):
    n_q = N_GROUPS * D_B
    B, S, _ = x.shape
    dils = [d for _, d in ATTN_PAIRS]
    q_scale = jnp.concatenate([jnp.full((n_q,), HEAD_DIM ** -0.5 * LOG2_E, F32), jnp.ones((D_B,), F32)])
    def kv_plan(wi):
        return ([(wi, grp * D_B, d, False) for grp, d in enumerate(dils)]
                + [(wi, n_q + grp * D_B, d, False) for grp, d in enumerate(dils)])

    def q_plan(wi):
        return [(wi, grp * D_B, d, False) for grp, d in enumerate(dils)] + [(wi, n_q, 1, True)]

    w_q = [(w_in_b[j] * q_scale).astype(BF16) for j in range(DEPTH - N_A_LAYERS)]
    h = x
    ks = vs = qs = gate = None
    for l in range(DEPTH):
        g = ln_g[l][None, :]
        b = ln_b[l][None, :]
        if l < N_A_LAYERS:
            h = _pool_layer(h, w_in_a[l].astype(BF16), w_grp_a[l].astype(BF16),
                            scale_a[l][None, :], w_out_a[l].astype(BF16), g, b)
            if l == N_A_LAYERS - 1:
                outs = _project(h, [w_kv.astype(BF16), w_q[0]], kv_plan(0) + q_plan(1))
                ks, vs = outs[:N_GROUPS], outs[N_GROUPS:2 * N_GROUPS]
                *qs, gate = outs[2 * N_GROUPS:]
        else:
            j = l - N_A_LAYERS
            outs, stats = zip(*[_attention_group(qs[grp], ks[grp], vs[grp], grp)
                                for grp in range(N_GROUPS)])
            last = l == DEPTH - 1
            h, *nxt = _merge_layer(outs, stats, gate.reshape(B, S, D_B), h, w_out_b[j].astype(BF16), g, b,
                                   *(() if last else (w_q[j + 1], q_plan(0))))
            if not last:
                *qs, gate = nxt
    return h
```
